```python
import jax, jax.numpy as jnp
from jax import lax
import numpy as np

D_MODEL = 1024
BATCH = 4
SEQ = 8192
DEPTH = 1

GMLP_GROUPS = 4
GMLP_GROUP_DIM = 128
GMLP_WIDTH = GMLP_GROUPS * GMLP_GROUP_DIM
CHUNK = 128
N_HEADS = 8
Q_RANK = 256
KV_RANK = 128
HEAD_V = 64
ATTN_WIDTH = N_HEADS * HEAD_V
IDX_HEADS = 4
IDX_DIM = 64
TOPK_MAX = 256
Q_BLOCK = 128
N_EXPERTS = 32
TOP_K = 4
D_EXPERT = 1024
SWIGLU_LIMIT = 7.0
SWIGLU_ALPHA = 1.702
MOE_BLOCK = 128
EPS = 1e-6

OFF_UV = 0
OFF_CQ = OFF_UV + 2 * GMLP_WIDTH
OFF_CKV = OFF_CQ + Q_RANK
OFF_KIDX = OFF_CKV + KV_RANK
OFF_WIDX = OFF_KIDX + IDX_DIM
OFF_GATE = OFF_WIDX + IDX_HEADS
IN_WIDTH = OFF_GATE + 2 * D_MODEL

kernel_name = "hybrid_gmlp_dsa_moe_block"


def rms_norm(x, g):
    xf = x.astype(jnp.float32)
    y = xf * lax.rsqrt(jnp.mean(xf * xf, axis=-1, keepdims=True) + EPS)
    return (y * g.astype(jnp.float32)).astype(x.dtype)


def gmlp_mixer(uv, v_g, w_s, b_s):
    B, S, _ = uv.shape
    z = jax.nn.gelu(uv)
    u, v = z[..., :GMLP_WIDTH], z[..., GMLP_WIDTH:]
    v = v.reshape(B, S // CHUNK, CHUNK, GMLP_GROUPS, GMLP_GROUP_DIM)
    v = rms_norm(v, v_g)
    causal = jnp.tril(jnp.ones((CHUNK, CHUNK), dtype=bool))
    w = jnp.where(causal[None], w_s, jnp.zeros_like(w_s)).astype(v.dtype)
    sv = jnp.einsum('gts,bcsgd->bctgd', w, v) + b_s.T[None, None, :, :, None].astype(v.dtype)
    return u * sv.reshape(B, S, GMLP_WIDTH)


def dsa_mixer(q, kv, q_idx, k_idx, w_idx, w_uv):
    B, S, H, C = q.shape
    nb = S // Q_BLOCK
    k_top = min(TOPK_MAX, S // 4)
    scale = C ** -0.5
    k_idx_f = k_idx.astype(jnp.float32)
    key_pos = jnp.arange(S)

    def to_blocks(a):
        return jnp.moveaxis(a.reshape((B, nb, Q_BLOCK) + a.shape[2:]), 1, 0)

    def attn_block(args):
        blk, qb, qib, wb = args
        t = blk * Q_BLOCK + jnp.arange(Q_BLOCK)
        logits = jnp.einsum('bqhd,bsd->bqhs', qib.astype(jnp.float32), k_idx_f) * (IDX_DIM ** -0.5)
        score = jnp.einsum('bqhs,bqh->bqs', jax.nn.relu(logits), wb.astype(jnp.float32))
        causal = key_pos[None, :] <= t[:, None]
        score = jnp.where(causal[None], score, -jnp.inf)
        _, idx = lax.top_k(score, k_top)
        valid = idx <= t[None, :, None]
        kv_sel = jax.vmap(lambda kv_b, i_b: kv_b[i_b])(kv, idx).astype(jnp.float32)
        s = jnp.einsum('bqhc,bqkc->bhqk', qb.astype(jnp.float32), kv_sel) * scale
        s = jnp.where(valid[:, None], s, -jnp.inf)
        p = jax.nn.softmax(s, axis=-1)
        o = jnp.einsum('bhqk,bqkc->bqhc', p, kv_sel)
        return o.astype(q.dtype)

    o = lax.map(attn_block, (jnp.arange(nb), to_blocks(q), to_blocks(q_idx), to_blocks(w_idx)))
    o = jnp.moveaxis(o, 0, 1).reshape(B, S, H, C)
    o = jnp.einsum('bshc,hcv->bshv', o, w_uv)
    return o.reshape(B, S, ATTN_WIDTH)


def moe(xn, w_router, b_router, w_exp1, b_exp1, w_exp2, b_exp2):
    B, S, D = xn.shape
    T = B * S
    xt = xn.reshape(T, D)
    logits = (xt @ w_router + b_router).astype(jnp.float32)
    top_val, top_idx = lax.top_k(logits, TOP_K)
    gates = jax.nn.softmax(top_val, axis=-1)
    P = T * TOP_K
    flat_e = top_idx.reshape(P)
    flat_tok = jnp.repeat(jnp.arange(T, dtype=jnp.int32), TOP_K)
    flat_g = gates.reshape(P)
    order = jnp.argsort(flat_e, stable=True)
    se = flat_e[order]
    counts = jnp.bincount(flat_e, length=N_EXPERTS)
    padded = (counts + MOE_BLOCK - 1) // MOE_BLOCK * MOE_BLOCK
    pad_end = jnp.cumsum(padded)
    pad_start = pad_end - padded
    grp_start = jnp.cumsum(counts) - counts
    dest = pad_start[se] + (jnp.arange(P) - grp_start[se])
    n_blocks = -(-P // MOE_BLOCK) + N_EXPERTS
    n_rows = n_blocks * MOE_BLOCK
    row_tok = jnp.full((n_rows,), T, dtype=jnp.int32).at[dest].set(flat_tok[order])
    row_gate = jnp.zeros((n_rows,), jnp.float32).at[dest].set(flat_g[order])
    blk_start = jnp.arange(n_blocks) * MOE_BLOCK
    blk_e = jnp.minimum(jnp.searchsorted(pad_end, blk_start, side='right'), N_EXPERTS - 1)
    x_pad = jnp.concatenate([xt, jnp.zeros((1, D), xt.dtype)], axis=0)

    def expert_block(args):
        tok, g, e = args
        xb = x_pad[tok]
        hb = xb @ w_exp1[e] + b_exp1[e]
        hg = jnp.minimum(hb[:, :D_EXPERT], SWIGLU_LIMIT)
        hu = jnp.clip(hb[:, D_EXPERT:], -SWIGLU_LIMIT, SWIGLU_LIMIT)
        act = (hu + 1.0) * (hg * jax.nn.sigmoid(SWIGLU_ALPHA * hg))
        y = act @ w_exp2[e] + b_exp2[e]
        return y * g[:, None].astype(y.dtype)

    ys = lax.map(expert_block, (row_tok.reshape(n_blocks, MOE_BLOCK),
                                row_gate.reshape(n_blocks, MOE_BLOCK), blk_e))
    out = jax.ops.segment_sum(ys.reshape(n_rows, D), row_tok, num_segments=T + 1)[:T]
    return out.reshape(B, S, D)


def setup_inputs(seed: int = 0) -> dict:
    key = jax.random.key(seed)
    ks = jax.random.split(key, 24)
    f32 = jnp.float32

    def nrm(k, shape, fan_in):
        return jax.random.normal(k, shape, f32) * (fan_in ** -0.5)

    def gain(k, n):
        return 1.0 + 0.1 * jax.random.normal(k, (n,), f32)

    return {
        "x": jax.random.normal(ks[0], (BATCH, SEQ, D_MODEL), f32),
        "norm1_g": gain(ks[1], D_MODEL),
        "w_in": nrm(ks[2], (D_MODEL, IN_WIDTH), D_MODEL),
        "q_lat_g": gain(ks[3], Q_RANK),
        "w_uq": nrm(ks[4], (Q_RANK, N_HEADS, KV_RANK), Q_RANK),
        "q_norm_g": gain(ks[5], KV_RANK),
        "kv_norm_g": gain(ks[6], KV_RANK),
        "w_uv": nrm(ks[7], (N_HEADS, KV_RANK, HEAD_V), KV_RANK),
        "w_q_idx": nrm(ks[8], (Q_RANK, IDX_HEADS, IDX_DIM), Q_RANK),
        "k_idx_g": gain(ks[9], IDX_DIM),
        "gmlp_v_g": gain(ks[10], GMLP_GROUP_DIM),
        "w_spatial": nrm(ks[11], (GMLP_GROUPS, CHUNK, CHUNK), CHUNK),
        "b_spatial": 1.0 + 0.1 * jax.random.normal(ks[12], (GMLP_GROUPS, CHUNK), f32),
        "w_proj_a": nrm(ks[13], (GMLP_WIDTH, D_MODEL), GMLP_WIDTH),
        "w_proj_b": nrm(ks[14], (ATTN_WIDTH, D_MODEL), ATTN_WIDTH),
        "w_out": nrm(ks[15], (D_MODEL, D_MODEL), D_MODEL),
        "norm2_g": gain(ks[16], D_MODEL),
        "w_router": nrm(ks[17], (D_MODEL, N_EXPERTS), D_MODEL),
        "b_router": 0.01 * jax.random.normal(ks[18], (N_EXPERTS,), f32),
        "w_exp1": nrm(ks[19], (N_EXPERTS, D_MODEL, 2 * D_EXPERT), D_MODEL),
        "b_exp1": 0.01 * jax.random.normal(ks[20], (N_EXPERTS, 2 * D_EXPERT), f32),
        "w_exp2": nrm(ks[21], (N_EXPERTS, D_EXPERT, D_MODEL), D_EXPERT),
        "b_exp2": 0.01 * jax.random.normal(ks[22], (N_EXPERTS, D_MODEL), f32),
    }


def reference(x, norm1_g, w_in, q_lat_g, w_uq, q_norm_g, kv_norm_g, w_uv, w_q_idx, k_idx_g,
              gmlp_v_g, w_spatial, b_spatial, w_proj_a, w_proj_b, w_out, norm2_g,
              w_router, b_router, w_exp1, b_exp1, w_exp2, b_exp2):
    for _ in range(DEPTH):
        xn = rms_norm(x, norm1_g)
        h = xn @ w_in
        y_a = gmlp_mixer(h[..., OFF_UV:OFF_CQ], gmlp_v_g, w_spatial, b_spatial)
        c_q = rms_norm(h[..., OFF_CQ:OFF_CKV], q_lat_g)
        q = rms_norm(jnp.einsum('bsr,rhc->bshc', c_q, w_uq), q_norm_g)
        kv = rms_norm(h[..., OFF_CKV:OFF_KIDX], kv_norm_g)
        q_idx = jnp.einsum('bsr,rhd->bshd', c_q, w_q_idx)
        k_idx = rms_norm(h[..., OFF_KIDX:OFF_WIDX], k_idx_g)
        w_idx = h[..., OFF_WIDX:OFF_GATE] * (IDX_HEADS ** -0.5)
        y_b = dsa_mixer(q, kv, q_idx, k_idx, w_idx, w_uv)
        g_a = jax.nn.sigmoid(h[..., OFF_GATE:OFF_GATE + D_MODEL])
        g_b = jax.nn.sigmoid(h[..., OFF_GATE + D_MODEL:IN_WIDTH])
        merged = g_a * (y_a @ w_proj_a) + g_b * (y_b @ w_proj_b)
        x = x + merged @ w_out
        x = x + moe(rms_norm(x, norm2_g), w_router, b_router, w_exp1, b_exp1, w_exp2, b_exp2)
    return x
```

```python
import functools
import math

import jax
import jax.numpy as jnp
from jax import lax
from jax.experimental import pallas as pl
from jax.experimental.pallas import tpu as pltpu

F32 = jnp.float32
BF16 = jnp.bfloat16
I32 = jnp.int32

D_MODEL = 1024
GMLP_GROUPS = 4
GMLP_GROUP_DIM = 128
GMLP_WIDTH = GMLP_GROUPS * GMLP_GROUP_DIM
CHUNK = 128
N_HEADS = 8
Q_RANK = 256
KV_RANK = 128
HEAD_V = 64
ATTN_WIDTH = N_HEADS * HEAD_V
IDX_HEADS = 4
IDX_DIM = 64
TOPK_MAX = 256
N_EXPERTS = 32
TOP_K = 4
D_EXPERT = 1024
SWIGLU_LIMIT = 7.0
SWIGLU_ALPHA = 1.702
EPS = 1e-6

OFF_UV = 0
OFF_CQ = OFF_UV + 2 * GMLP_WIDTH
OFF_CKV = OFF_CQ + Q_RANK
OFF_KIDX = OFF_CKV + KV_RANK
OFF_WIDX = OFF_KIDX + IDX_DIM
OFF_GATE = OFF_WIDX + IDX_HEADS
IN_WIDTH = OFF_GATE + 2 * D_MODEL

LANES = 128
Q_BLOCK = 128
VMEM_LIMIT_BYTES = 56 * 1024 * 1024

TM_INPROJ = 512
KEY_BLOCK = 512
TM_MERGE = 512
TM_ROWS = 256
MOE_BLOCK = 256

INT_MIN = -(2 ** 31)
NEG_BIG = -1e30
LOG2E = 1.4426950408889634


def _full_spec(arr):
    nd = arr.ndim
    return pl.BlockSpec(arr.shape, lambda *_: (0,) * nd)


def _rms(v, axis=-1):
    return v * lax.rsqrt(jnp.mean(v * v, axis=axis, keepdims=True) + EPS)


def _gelu_tanh(v):
    c = math.sqrt(2.0 / math.pi)
    return 0.5 * v * (1.0 + jnp.tanh(c * (v + 0.044715 * (v * v * v))))


def _inproj_kernel(x_ref, g1_ref, wuv_ref, wcq_ref, wmisc_ref, wga_ref, wgb_ref,
                   qlat_g_ref, wuq_ref, qn_g_ref, kvn_g_ref, wqi_ref, kidx_g_ref,
                   vg_ref, wsp_ref, bsp_ref, wpa_ref,
                   pa_ref, gb_ref, q_ref, qi_ref, kv_ref, ki_ref, wi_ref,
                   ya_ref):
    tm = x_ref.shape[0]
    x = x_ref[...]
    xn = (_rms(x) * g1_ref[...]).astype(BF16)

    z = _gelu_tanh(jnp.dot(xn, wuv_ref[...], preferred_element_type=F32))
    r_io = lax.broadcasted_iota(I32, (CHUNK, CHUNK), 0)
    c_io = lax.broadcasted_iota(I32, (CHUNK, CHUNK), 1)
    causal = c_io <= r_io
    for g in range(GMLP_GROUPS):
        cols = slice(g * GMLP_GROUP_DIM, (g + 1) * GMLP_GROUP_DIM)
        u_g = z[:, cols]
        v_g = z[:, GMLP_WIDTH + g * GMLP_GROUP_DIM:GMLP_WIDTH + (g + 1) * GMLP_GROUP_DIM]
        vn_g = (_rms(v_g) * vg_ref[...]).astype(BF16)
        w_g = jnp.where(causal, wsp_ref[g], 0.0).astype(BF16)
        b_g = bsp_ref[:, g:g + 1]
        for c in range(tm // CHUNK):
            rows = slice(c * CHUNK, (c + 1) * CHUNK)
            sv = jnp.dot(w_g, vn_g[rows], preferred_element_type=F32) + b_g
            ya_ref[rows, cols] = (u_g[rows] * sv).astype(BF16)
    gate_a = jax.nn.sigmoid(jnp.dot(xn, wga_ref[...], preferred_element_type=F32))
    pa = gate_a * jnp.dot(ya_ref[...], wpa_ref[...], preferred_element_type=F32)
    pa_ref[...] = pa.astype(BF16)
    gb_ref[...] = jax.nn.sigmoid(
        jnp.dot(xn, wgb_ref[...], preferred_element_type=F32)).astype(BF16)

    c_q = (_rms(jnp.dot(xn, wcq_ref[...], preferred_element_type=F32))
           * qlat_g_ref[...]).astype(BF16)
    qf = jnp.dot(c_q, wuq_ref[...], preferred_element_type=F32)
    qif = jnp.dot(c_q, wqi_ref[...], preferred_element_type=F32)
    for h in range(N_HEADS):
        qh = (_rms(qf[:, h * KV_RANK:(h + 1) * KV_RANK]) * qn_g_ref[...]).astype(BF16)
        for j in range(tm // Q_BLOCK):
            q_ref[j, h] = qh[j * Q_BLOCK:(j + 1) * Q_BLOCK]
    for h in range(IDX_HEADS):
        qih = qif[:, h * LANES:(h + 1) * LANES].astype(BF16)
        for j in range(tm // Q_BLOCK):
            qi_ref[j, h] = qih[j * Q_BLOCK:(j + 1) * Q_BLOCK]

    hm = jnp.dot(xn, wmisc_ref[...], preferred_element_type=F32)
    kv_ref[...] = (_rms(hm[:, :KV_RANK]) * kvn_g_ref[...]).astype(BF16)
    hk = hm[:, KV_RANK:]
    lane = lax.broadcasted_iota(I32, hk.shape, 1)
    is_k = lane < IDX_DIM
    ms_k = jnp.sum(jnp.where(is_k, hk * hk, 0.0), axis=-1, keepdims=True) * (1.0 / IDX_DIM)
    ki = jnp.where(is_k, hk * lax.rsqrt(ms_k + EPS) * kidx_g_ref[...], 0.0)
    ki_ref[...] = ki.astype(BF16)
    wi_ref[...] = hk * (IDX_HEADS ** -0.5)


def _inproj(x2d, g1, wuv, wcq, wmisc, wga, wgb, qlat_g, wuq, qn_g, kvn_g, wqi, kidx_g,
            vg, wsp, bsp_t, wpa):
    t, d = x2d.shape
    tm = min(TM_INPROJ, t)
    nqb = t // Q_BLOCK
    weights = (g1, wuv, wcq, wmisc, wga, wgb, qlat_g, wuq, qn_g, kvn_g, wqi, kidx_g,
               vg, wsp, bsp_t, wpa)
    out_shape = (
        jax.ShapeDtypeStruct((t, d), BF16),
        jax.ShapeDtypeStruct((t, d), BF16),
        jax.ShapeDtypeStruct((nqb, N_HEADS, Q_BLOCK, KV_RANK), BF16),
        jax.ShapeDtypeStruct((nqb, IDX_HEADS, Q_BLOCK, LANES), BF16),
        jax.ShapeDtypeStruct((t, KV_RANK), BF16),
        jax.ShapeDtypeStruct((t, LANES), BF16),
        jax.ShapeDtypeStruct((t, LANES), F32),
    )
    row = lambda w: pl.BlockSpec((tm, w), lambda i: (i, 0))
    out_specs = (
        row(d), row(d),
        pl.BlockSpec((tm // Q_BLOCK, N_HEADS, Q_BLOCK, KV_RANK), lambda i: (i, 0, 0, 0)),
        pl.BlockSpec((tm // Q_BLOCK, IDX_HEADS, Q_BLOCK, LANES), lambda i: (i, 0, 0, 0)),
        row(KV_RANK), row(LANES), row(LANES),
    )
    return pl.pallas_call(
        _inproj_kernel,
        grid=(t // tm,),
        in_specs=[row(d)] + [_full_spec(w) for w in weights],
        out_specs=out_specs,
        out_shape=out_shape,
        scratch_shapes=[pltpu.VMEM((tm, GMLP_WIDTH), BF16)],
        compiler_params=pltpu.CompilerParams(
            dimension_semantics=("arbitrary",), vmem_limit_bytes=VMEM_LIMIT_BYTES),
        name="inproj",
    )(x2d, *weights)


def _dsa_kernel(q_ref, qi_ref, wi_ref, kv_ref, ki_ref, wuvp_ref, yb_ref,
                keys_ref, m_ref, l_ref, alpha_ref, acc_ref, p_ref, *, kb, k_top):
    qb = pl.program_id(1)
    q0 = qb * Q_BLOCK
    nkb = (q0 + Q_BLOCK + kb - 1) // kb
    nt = (((1,), (1,)), ((), ()))

    qidx = qi_ref[0].reshape(IDX_HEADS * Q_BLOCK, LANES)
    wslab = wi_ref[...]
    w_cols = [wslab[:, IDX_DIM + h:IDX_DIM + h + 1] for h in range(IDX_HEADS)]
    row_t = q0 + lax.broadcasted_iota(I32, (Q_BLOCK, 1), 0)

    def score_block(j, carry):
        kblk = ki_ref[0, pl.ds(pl.multiple_of(j * kb, kb), kb), :]
        lg = lax.dot_general(qidx, kblk, nt, preferred_element_type=F32)
        sc = jnp.zeros((Q_BLOCK, kb), F32)
        for h in range(IDX_HEADS):
            sc = sc + jnp.maximum(lg[h * Q_BLOCK:(h + 1) * Q_BLOCK], 0.0) * w_cols[h]
        sc = jnp.where(sc == 0.0, 0.0, sc)
        bits = pltpu.bitcast(sc, I32)
        key = bits ^ ((bits >> 31) & 0x7FFFFFFF)
        kpos = j * kb + lax.broadcasted_iota(I32, (1, kb), 1)
        keys_ref[j] = jnp.where(kpos <= row_t, key, INT_MIN)
        return carry

    lax.fori_loop(0, nkb, score_block, 0)

    def count(cmp, thr):
        thr_b = jnp.broadcast_to(thr, (Q_BLOCK, LANES))

        def body(j, acc):
            k = keys_ref[j]
            for c in range(kb // LANES):
                acc = acc + jnp.where(cmp(k[:, c * LANES:(c + 1) * LANES], thr_b), 1.0, 0.0)
            return acc

        acc = lax.fori_loop(0, nkb, body, jnp.zeros((Q_BLOCK, LANES), F32))
        return jnp.sum(acc, axis=1, keepdims=True)

    ge = lambda a, b: a >= b
    gt = lambda a, b: a > b

    def bit_step(i, thr):
        cand = thr + lax.shift_left(jnp.int32(1), 31 - i)
        return jnp.where(count(ge, cand) >= k_top, cand, thr)

    thr = lax.fori_loop(0, 32, bit_step, jnp.full((Q_BLOCK, 1), INT_MIN, I32))
    thr = jnp.maximum(thr, INT_MIN + 1)
    has_ties = jnp.max(count(ge, thr)) > k_top
    need = lax.cond(has_ties, lambda: k_top - count(gt, thr),
                    lambda: jnp.zeros((Q_BLOCK, 1), F32))
    thr_kb = jnp.broadcast_to(thr, (Q_BLOCK, kb))

    def select_ties(k, run):
        eq = k == thr_kb
        r_io = lax.broadcasted_iota(I32, (kb, kb), 0)
        c_io = lax.broadcasted_iota(I32, (kb, kb), 1)
        upper = jnp.where(r_io <= c_io, 1.0, 0.0).astype(BF16)
        pref = jnp.dot(jnp.where(eq, 1.0, 0.0).astype(BF16), upper,
                       preferred_element_type=F32)
        sel = (k > thr_kb) | (eq & (pref + run <= need))
        return jnp.where(sel, 1, 0), run + pref[:, kb - 1:kb]

    def select_plain(k, run):
        return jnp.where(k >= thr_kb, 1, 0), run

    qh = q_ref[0].reshape(N_HEADS * Q_BLOCK, KV_RANK)
    m_ref[...] = jnp.full(m_ref.shape, NEG_BIG, F32)
    l_ref[...] = jnp.zeros(l_ref.shape, F32)
    acc_ref[...] = jnp.zeros(acc_ref.shape, F32)

    def attn_block(j, run):
        k = keys_ref[j]
        kvb = kv_ref[0, pl.ds(pl.multiple_of(j * kb, kb), kb), :]
        sel_i, run = lax.cond(has_ties, select_ties, select_plain, k, run)
        sel = sel_i != 0
        logits = lax.dot_general(qh, kvb, nt, preferred_element_type=F32)
        for h in range(N_HEADS):
            rows = slice(h * Q_BLOCK, (h + 1) * Q_BLOCK)
            s = jnp.where(sel, logits[rows], NEG_BIG)
            m_old = m_ref[rows]
            m_new = jnp.maximum(m_old, jnp.max(s, axis=1, keepdims=True))
            alpha = jnp.exp2(m_old - m_new)
            p = jnp.exp2(s - m_new)
            l_ref[rows] = alpha * l_ref[rows] + jnp.sum(p, axis=1, keepdims=True)
            m_ref[rows] = m_new
            alpha_ref[rows] = alpha
            p_ref[rows] = p.astype(BF16)
        pv = jnp.dot(p_ref[...], kvb, preferred_element_type=F32)
        acc_ref[...] = alpha_ref[...] * acc_ref[...] + pv
        return run

    lax.fori_loop(0, nkb, attn_block, jnp.zeros((Q_BLOCK, 1), F32))

    o = (acc_ref[...] / l_ref[...]).astype(BF16)
    y = jnp.zeros((Q_BLOCK, ATTN_WIDTH), F32)
    for h in range(N_HEADS):
        y = y + jnp.dot(o[h * Q_BLOCK:(h + 1) * Q_BLOCK], wuvp_ref[h],
                        preferred_element_type=F32)
    yb_ref[...] = y.astype(BF16)


def _dsa(q_hm, qi_hm, wi, kv, ki, wuvp, batch, seq):
    nq = seq // Q_BLOCK
    kb = min(KEY_BLOCK, seq)
    k_top = min(TOPK_MAX, seq // 4)
    rows = N_HEADS * Q_BLOCK
    kern = functools.partial(_dsa_kernel, kb=kb, k_top=float(k_top))
    return pl.pallas_call(
        kern,
        grid=(batch, nq),
        in_specs=[
            pl.BlockSpec((1, N_HEADS, Q_BLOCK, KV_RANK), lambda b, i: (b * nq + i, 0, 0, 0)),
            pl.BlockSpec((1, IDX_HEADS, Q_BLOCK, LANES), lambda b, i: (b * nq + i, 0, 0, 0)),
            pl.BlockSpec((Q_BLOCK, LANES), lambda b, i: (b * nq + i, 0)),
            pl.BlockSpec((1, seq, KV_RANK), lambda b, i: (b, 0, 0)),
            pl.BlockSpec((1, seq, LANES), lambda b, i: (b, 0, 0)),
            _full_spec(wuvp),
        ],
        out_specs=pl.BlockSpec((Q_BLOCK, ATTN_WIDTH), lambda b, i: (b * nq + i, 0)),
        out_shape=jax.ShapeDtypeStruct((batch * seq, ATTN_WIDTH), BF16),
        scratch_shapes=[
            pltpu.VMEM((seq // kb, Q_BLOCK, kb), I32),
            pltpu.VMEM((rows, 1), F32),
            pltpu.VMEM((rows, 1), F32),
            pltpu.VMEM((rows, 1), F32),
            pltpu.VMEM((rows, KV_RANK), F32),
            pltpu.VMEM((rows, kb), BF16),
        ],
        compiler_params=pltpu.CompilerParams(
            dimension_semantics=("arbitrary", "arbitrary"), vmem_limit_bytes=VMEM_LIMIT_BYTES),
        name="dsa",
    )(q_hm, qi_hm, wi, kv.reshape(batch, seq, KV_RANK), ki.reshape(batch, seq, LANES), wuvp)


def _merge_kernel(x_ref, pa_ref, gb_ref, yb_ref, wpb_ref, wout_ref, g2_ref, wr_ref, br_ref,
                  x2_ref, xn2_ref, te_ref, gate_ref, rank_ref, cnt_ref, carry_ref):
    tm = x_ref.shape[0]

    @pl.when(pl.program_id(0) == 0)
    def _():
        carry_ref[...] = jnp.zeros(carry_ref.shape, F32)

    pb = jnp.dot(yb_ref[...], wpb_ref[...], preferred_element_type=F32)
    merged = pa_ref[...].astype(F32) + gb_ref[...].astype(F32) * pb
    x2 = x_ref[...] + jnp.dot(merged.astype(BF16), wout_ref[...], preferred_element_type=F32)
    x2_ref[...] = x2
    xn2 = _rms(x2) * g2_ref[...]
    xn2_ref[...] = xn2

    logits = jnp.dot(xn2, wr_ref[...], preferred_element_type=F32,
                     precision=lax.Precision.HIGHEST) + br_ref[...]
    lane = lax.broadcasted_iota(I32, (tm, LANES), 1).astype(F32)
    work = jnp.where(lane < N_EXPERTS, logits, -jnp.inf)
    onehot = jnp.zeros((tm, LANES), F32)
    vals, idxs = [], []
    for _ in range(TOP_K):
        mx = jnp.max(work, axis=1, keepdims=True)
        ix = jnp.min(jnp.where(work == mx, lane, float(LANES)), axis=1, keepdims=True)
        hit = lane == ix
        onehot = onehot + jnp.where(hit, 1.0, 0.0)
        work = jnp.where(hit, -jnp.inf, work)
        vals.append(mx)
        idxs.append(ix)
    exps = [jnp.exp(v - vals[0]) for v in vals]
    denom = exps[0] + exps[1] + exps[2] + exps[3]

    r_io = lax.broadcasted_iota(I32, (tm, tm), 0)
    c_io = lax.broadcasted_iota(I32, (tm, tm), 1)
    lower = jnp.where(c_io <= r_io, 1.0, 0.0).astype(BF16)
    pref = jnp.dot(lower, onehot.astype(BF16), preferred_element_type=F32)
    base = pref - onehot + carry_ref[...]
    te = jnp.zeros((tm, LANES), F32)
    gate = jnp.zeros((tm, LANES), F32)
    rank = jnp.zeros((tm, LANES), F32)
    for k in range(TOP_K):
        slot = lane == float(k)
        rk = jnp.sum(jnp.where(lane == idxs[k], base, 0.0), axis=1, keepdims=True)
        te = jnp.where(slot, idxs[k], te)
        gate = jnp.where(slot, exps[k] / denom, gate)
        rank = jnp.where(slot, rk, rank)
    te_ref[...] = te.astype(I32)
    gate_ref[...] = gate
    rank_ref[...] = rank.astype(I32)
    carry_ref[...] = carry_ref[...] + pref[tm - 1:tm, :]
    cnt_ref[...] = carry_ref[...]


def _merge(x2d, pa, gb, yb, wpb, wout, g2, wr, br):
    t, d = x2d.shape
    tm = min(TM_MERGE, t)
    row = lambda w: pl.BlockSpec((tm, w), lambda i: (i, 0))
    weights = (wpb, wout, g2, wr, br)
    out_shape = (
        jax.ShapeDtypeStruct((t, d), F32),
        jax.ShapeDtypeStruct((t, d), F32),
        jax.ShapeDtypeStruct((t, LANES), I32),
        jax.ShapeDtypeStruct((t, LANES), F32),
        jax.ShapeDtypeStruct((t, LANES), I32),
        jax.ShapeDtypeStruct((1, LANES), F32),
    )
    return pl.pallas_call(
        _merge_kernel,
        grid=(t // tm,),
        in_specs=[row(d), row(d), row(d), row(ATTN_WIDTH)] + [_full_spec(w) for w in weights],
        out_specs=(row(d), row(d), row(LANES), row(LANES), row(LANES),
                   pl.BlockSpec((1, LANES), lambda i: (0, 0))),
        out_shape=out_shape,
        scratch_shapes=[pltpu.VMEM((1, LANES), F32)],
        compiler_params=pltpu.CompilerParams(
            dimension_semantics=("arbitrary",), vmem_limit_bytes=VMEM_LIMIT_BYTES),
        name="merge",
    )(x2d, pa, gb, yb, *weights)


def _row_copy(src, src_row, dst, dst_row, sem):
    return pltpu.make_async_copy(src.at[pl.ds(src_row, 1)], dst.at[pl.ds(dst_row, 1)], sem)


def _dispatch_kernel(pstart_ref, te_ref, rank_ref, xn2_ref, xs_in_ref, xs_ref, sem):
    del xs_in_ref
    tm = xn2_ref.shape[0]

    def issue(t, carry):
        for k in range(TOP_K):
            dest = pstart_ref[te_ref[t * TOP_K + k]] + rank_ref[t * TOP_K + k]
            _row_copy(xn2_ref, t, xs_ref, dest, sem).start()
        return carry

    lax.fori_loop(0, tm, issue, 0)

    def drain(t, carry):
        for k in range(TOP_K):
            _row_copy(xn2_ref, 0, xs_ref, 0, sem).wait()
        return carry

    lax.fori_loop(0, tm, drain, 0)


def _dispatch(pstart, te_flat, rank_flat, xn2, n_rows):
    t, d = xn2.shape
    tm = min(TM_ROWS, t)
    smem_blk = pl.BlockSpec((tm * TOP_K,), lambda i, ps: (i,), memory_space=pltpu.SMEM)
    grid_spec = pltpu.PrefetchScalarGridSpec(
        num_scalar_prefetch=1,
        grid=(t // tm,),
        in_specs=[smem_blk, smem_blk,
                  pl.BlockSpec((tm, d), lambda i, ps: (i, 0)),
                  pl.BlockSpec(memory_space=pl.ANY)],
        out_specs=pl.BlockSpec(memory_space=pl.ANY),
        scratch_shapes=[pltpu.SemaphoreType.DMA(())],
    )
    return pl.pallas_call(
        _dispatch_kernel,
        grid_spec=grid_spec,
        out_shape=jax.ShapeDtypeStruct((n_rows, d), F32),
        input_output_aliases={4: 0},
        compiler_params=pltpu.CompilerParams(dimension_semantics=("arbitrary",)),
        name="dispatch",
    )(pstart, te_flat, rank_flat, xn2, jnp.zeros((n_rows, d), F32))


def _moe_kernel(blk_e_ref, xs_ref, w1_ref, b1_ref, w2_ref, b2_ref, y_ref):
    del blk_e_ref
    xb = xs_ref[...].astype(BF16)
    hb = jnp.dot(xb, w1_ref[0], preferred_element_type=F32) + b1_ref[0]
    hg = jnp.minimum(hb[:, :D_EXPERT], SWIGLU_LIMIT)
    hu = jnp.clip(hb[:, D_EXPERT:], -SWIGLU_LIMIT, SWIGLU_LIMIT)
    act = (hu + 1.0) * (hg * jax.nn.sigmoid(SWIGLU_ALPHA * hg))
    y_ref[...] = jnp.dot(act.astype(BF16), w2_ref[0], preferred_element_type=F32) + b2_ref[0]


def _moe(blk_e, xs, w1, b1, w2, b2):
    n_rows, d = xs.shape
    n_blocks = n_rows // MOE_BLOCK
    grid_spec = pltpu.PrefetchScalarGridSpec(
        num_scalar_prefetch=1,
        grid=(n_blocks,),
        in_specs=[
            pl.BlockSpec((MOE_BLOCK, d), lambda i, be: (i, 0)),
            pl.BlockSpec((1, d, 2 * D_EXPERT), lambda i, be: (be[i], 0, 0)),
            pl.BlockSpec((1, 1, 2 * D_EXPERT), lambda i, be: (be[i], 0, 0)),
            pl.BlockSpec((1, D_EXPERT, d), lambda i, be: (be[i], 0, 0)),
            pl.BlockSpec((1, 1, d), lambda i, be: (be[i], 0, 0)),
        ],
        out_specs=pl.BlockSpec((MOE_BLOCK, d), lambda i, be: (i, 0)),
    )
    return pl.pallas_call(
        _moe_kernel,
        grid_spec=grid_spec,
        out_shape=jax.ShapeDtypeStruct((n_rows, d), F32),
        compiler_params=pltpu.CompilerParams(
            dimension_semantics=("arbitrary",), vmem_limit_bytes=VMEM_LIMIT_BYTES),
        name="moe",
    )(blk_e, xs, w1, b1, w2, b2)


def _combine_kernel(pstart_ref, te_ref, rank_ref, x2_ref, gate_ref, ys_ref, out_ref, buf_ref, sem):
    tm = x2_ref.shape[0]

    def issue(t, carry):
        for k in range(TOP_K):
            src = pstart_ref[te_ref[t * TOP_K + k]] + rank_ref[t * TOP_K + k]
            _row_copy(ys_ref, src, buf_ref.at[k], t, sem).start()
        return carry

    lax.fori_loop(0, tm, issue, 0)

    def drain(t, carry):
        for k in range(TOP_K):
            _row_copy(ys_ref, 0, buf_ref.at[k], 0, sem).wait()
        return carry

    lax.fori_loop(0, tm, drain, 0)

    gate = gate_ref[...]
    out = x2_ref[...]
    for k in range(TOP_K):
        out = out + gate[:, k:k + 1] * buf_ref[k]
    out_ref[...] = out


def _combine(pstart, te_flat, rank_flat, x2, gate, ys):
    t, d = x2.shape
    tm = min(TM_ROWS, t)
    smem_blk = pl.BlockSpec((tm * TOP_K,), lambda i, ps: (i,), memory_space=pltpu.SMEM)
    grid_spec = pltpu.PrefetchScalarGridSpec(
        num_scalar_prefetch=1,
        grid=(t // tm,),
        in_specs=[smem_blk, smem_blk,
                  pl.BlockSpec((tm, d), lambda i, ps: (i, 0)),
                  pl.BlockSpec((tm, LANES), lambda i, ps: (i, 0)),
                  pl.BlockSpec(memory_space=pl.ANY)],
        out_specs=pl.BlockSpec((tm, d), lambda i, ps: (i, 0)),
        scratch_shapes=[pltpu.VMEM((TOP_K, tm, d), F32), pltpu.SemaphoreType.DMA(())],
    )
    return pl.pallas_call(
        _combine_kernel,
        grid_spec=grid_spec,
        out_shape=jax.ShapeDtypeStruct((t, d), F32),
        compiler_params=pltpu.CompilerParams(
            dimension_semantics=("arbitrary",), vmem_limit_bytes=VMEM_LIMIT_BYTES),
        name="combine",
    )(pstart, te_flat, rank_flat, x2, gate, ys)


def kernel(x, norm1_g, w_in, q_lat_g, w_uq, q_norm_g, kv_norm_g, w_uv, w_q_idx, k_idx_g,
           gmlp_v_g, w_spatial, b_spatial, w_proj_a, w_proj_b, w_out, norm2_g,
           w_router, b_router, w_exp1, b_exp1, w_exp2, b_exp2):
    batch, seq, d = x.shape
    t = batch * seq
    assert d == D_MODEL and seq % min(KEY_BLOCK, seq) == 0 and seq % Q_BLOCK == 0
    assert t % TM_INPROJ == 0 and t % TM_MERGE == 0 and (t * TOP_K) % MOE_BLOCK == 0
    x2d = x.reshape(t, d)
    row = lambda v: v.reshape(1, -1).astype(F32)

    wuv = w_in[:, OFF_UV:OFF_CQ].astype(BF16)
    wcq = w_in[:, OFF_CQ:OFF_CKV].astype(BF16)
    wmisc = jnp.pad(w_in[:, OFF_CKV:OFF_GATE], ((0, 0), (0, 2 * LANES - (OFF_GATE - OFF_CKV)))
                    ).astype(BF16)
    wga = w_in[:, OFF_GATE:OFF_GATE + D_MODEL].astype(BF16)
    wgb = w_in[:, OFF_GATE + D_MODEL:IN_WIDTH].astype(BF16)
    wuq = w_uq.reshape(Q_RANK, N_HEADS * KV_RANK).astype(BF16)
    wqi = jnp.pad(w_q_idx * (IDX_DIM ** -0.5), ((0, 0), (0, 0), (0, LANES - IDX_DIM))
                  ).reshape(Q_RANK, IDX_HEADS * LANES).astype(BF16)
    kidx_g = jnp.pad(k_idx_g, (0, LANES - IDX_DIM)).reshape(1, LANES).astype(F32)
    qn_g = row(q_norm_g) * (KV_RANK ** -0.5 * LOG2E)
    wuvp = jnp.zeros((N_HEADS, KV_RANK, ATTN_WIDTH), F32)
    for h in range(N_HEADS):
        wuvp = wuvp.at[h, :, h * HEAD_V:(h + 1) * HEAD_V].set(w_uv[h])
    wuvp = wuvp.astype(BF16)

    pa, gb, q_hm, qi_hm, kv, ki, wi = _inproj(
        x2d, row(norm1_g), wuv, wcq, wmisc, wga, wgb, row(q_lat_g), wuq, qn_g,
        row(kv_norm_g), wqi, kidx_g, row(gmlp_v_g), w_spatial.astype(F32),
        b_spatial.T.astype(F32), w_proj_a.astype(BF16))

    yb = _dsa(q_hm, qi_hm, wi, kv, ki, wuvp, batch, seq)

    wr = jnp.pad(w_router, ((0, 0), (0, LANES - N_EXPERTS))).astype(F32)
    br = jnp.pad(b_router, (0, LANES - N_EXPERTS)).reshape(1, LANES).astype(F32)
    x2, xn2, te, gate, rank, cnt = _merge(
        x2d, pa, gb, yb, w_proj_b.astype(BF16), w_out.astype(BF16), row(norm2_g), wr, br)

    counts = cnt[0, :N_EXPERTS].astype(I32)
    padded = (counts + MOE_BLOCK - 1) // MOE_BLOCK * MOE_BLOCK
    pad_end = jnp.cumsum(padded)
    pstart = (pad_end - padded).astype(I32)
    n_blocks = (t * TOP_K) // MOE_BLOCK + N_EXPERTS
    blk_e = jnp.minimum(
        jnp.searchsorted(pad_end, jnp.arange(n_blocks, dtype=I32) * MOE_BLOCK, side="right"),
        N_EXPERTS - 1).astype(I32)
    te_flat = te[:, :TOP_K].reshape(t * TOP_K)
    rank_flat = rank[:, :TOP_K].reshape(t * TOP_K)

    xs = _dispatch(pstart, te_flat, rank_flat, xn2, n_blocks * MOE_BLOCK)
    ys = _moe(blk_e, xs, w_exp1.astype(BF16), b_exp1.reshape(N_EXPERTS, 1, -1).astype(F32),
              w_exp2.astype(BF16), b_exp2.reshape(N_EXPERTS, 1, -1).astype(F32))
    out = _combine(pstart, te_flat, rank_flat, x2, gate, ys)
    return out.reshape(batch, seq, d)
```

```python
import functools
import math

import jax
import jax.numpy as jnp
from jax import lax
from jax.experimental import pallas as pl
from jax.experimental.pallas import tpu as pltpu

F32 = jnp.float32
BF16 = jnp.bfloat16
I32 = jnp.int32

D_MODEL = 1024
GMLP_GROUPS = 4
GMLP_GROUP_DIM = 128
GMLP_WIDTH = GMLP_GROUPS * GMLP_GROUP_DIM
CHUNK = 128
N_HEADS = 8
Q_RANK = 256
KV_RANK = 128
HEAD_V = 64
ATTN_WIDTH = N_HEADS * HEAD_V
IDX_HEADS = 4
IDX_DIM = 64
TOPK_MAX = 256
N_EXPERTS = 32
TOP_K = 4
D_EXPERT = 1024
SWIGLU_LIMIT = 7.0
SWIGLU_ALPHA = 1.702
EPS = 1e-6

OFF_UV = 0
OFF_CQ = OFF_UV + 2 * GMLP_WIDTH
OFF_CKV = OFF_CQ + Q_RANK
OFF_KIDX = OFF_CKV + KV_RANK
OFF_WIDX = OFF_KIDX + IDX_DIM
OFF_GATE = OFF_WIDX + IDX_HEADS
IN_WIDTH = OFF_GATE + 2 * D_MODEL

LANES = 128
Q_BLOCK = 128
VMEM_LIMIT_BYTES = 56 * 1024 * 1024

TM_INPROJ = 512
KEY_BLOCK = 512
HEAD_GROUP = 2
TM_MERGE = 512
TM_ROWS = 256
MOE_BLOCK = 256

INT_MIN = -(2 ** 31)
NEG_BIG = -1e30
LOG2E = 1.4426950408889634


def _full_spec(arr):
    nd = arr.ndim
    return pl.BlockSpec(arr.shape, lambda *_: (0,) * nd)


def _rms(v, axis=-1):
    return v * lax.rsqrt(jnp.mean(v * v, axis=axis, keepdims=True) + EPS)


def _gelu_tanh(v):
    c = math.sqrt(2.0 / math.pi)
    return 0.5 * v * (1.0 + jnp.tanh(c * (v + 0.044715 * (v * v * v))))


def _inproj_kernel(x_ref, g1_ref, wuv_ref, wcq_ref, wmisc_ref, wga_ref, wgb_ref,
                   qlat_g_ref, wuq_ref, qn_g_ref, kvn_g_ref, wqi_ref, kidx_g_ref,
                   vg_ref, wsp_ref, bsp_ref, wpa_ref,
                   pa_ref, gb_ref, q_ref, qi_ref, kv_ref, ki_ref, wi_ref,
                   ya_ref):
    tm = x_ref.shape[0]
    x = x_ref[...]
    xn = (_rms(x) * g1_ref[...]).astype(BF16)

    z = _gelu_tanh(jnp.dot(xn, wuv_ref[...], preferred_element_type=F32))
    r_io = lax.broadcasted_iota(I32, (CHUNK, CHUNK), 0)
    c_io = lax.broadcasted_iota(I32, (CHUNK, CHUNK), 1)
    causal = c_io <= r_io
    for g in range(GMLP_GROUPS):
        cols = slice(g * GMLP_GROUP_DIM, (g + 1) * GMLP_GROUP_DIM)
        u_g = z[:, cols]
        v_g = z[:, GMLP_WIDTH + g * GMLP_GROUP_DIM:GMLP_WIDTH + (g + 1) * GMLP_GROUP_DIM]
        vn_g = (_rms(v_g) * vg_ref[...]).astype(BF16)
        w_g = jnp.where(causal, wsp_ref[g], 0.0).astype(BF16)
        b_g = bsp_ref[:, g:g + 1]
        for c in range(tm // CHUNK):
            rows = slice(c * CHUNK, (c + 1) * CHUNK)
            sv = jnp.dot(w_g, vn_g[rows], preferred_element_type=F32) + b_g
            ya_ref[rows, cols] = (u_g[rows] * sv).astype(BF16)
    gate_a = jax.nn.sigmoid(jnp.dot(xn, wga_ref[...], preferred_element_type=F32))
    pa = gate_a * jnp.dot(ya_ref[...], wpa_ref[...], preferred_element_type=F32)
    pa_ref[...] = pa.astype(BF16)
    gb_ref[...] = jax.nn.sigmoid(
        jnp.dot(xn, wgb_ref[...], preferred_element_type=F32)).astype(BF16)

    c_q = (_rms(jnp.dot(xn, wcq_ref[...], preferred_element_type=F32))
           * qlat_g_ref[...]).astype(BF16)
    qf = jnp.dot(c_q, wuq_ref[...], preferred_element_type=F32)
    qif = jnp.dot(c_q, wqi_ref[...], preferred_element_type=F32)
    for h in range(N_HEADS):
        qh = (_rms(qf[:, h * KV_RANK:(h + 1) * KV_RANK]) * qn_g_ref[...]).astype(BF16)
        for j in range(tm // Q_BLOCK):
            q_ref[j, h] = qh[j * Q_BLOCK:(j + 1) * Q_BLOCK]
    for h in range(IDX_HEADS):
        qih = qif[:, h * LANES:(h + 1) * LANES].astype(BF16)
        for j in range(tm // Q_BLOCK):
            qi_ref[j, h] = qih[j * Q_BLOCK:(j + 1) * Q_BLOCK]

    hm = jnp.dot(xn, wmisc_ref[...], preferred_element_type=F32)
    hk = hm[:, KV_RANK:]
    lane = lax.broadcasted_iota(I32, hk.shape, 1)
    kv_ref[:, :KV_RANK] = (_rms(hm[:, :KV_RANK]) * kvn_g_ref[...]).astype(BF16)
    kv_ref[:, KV_RANK:] = jnp.where(lane == 0, 1.0, 0.0).astype(BF16)
    is_k = lane < IDX_DIM
    ms_k = jnp.sum(jnp.where(is_k, hk * hk, 0.0), axis=-1, keepdims=True) * (1.0 / IDX_DIM)
    ki = jnp.where(is_k, hk * lax.rsqrt(ms_k + EPS) * kidx_g_ref[...], 0.0)
    ki_ref[...] = ki.astype(BF16)
    wi_ref[...] = hk * (IDX_HEADS ** -0.5)


def _inproj(x2d, g1, wuv, wcq, wmisc, wga, wgb, qlat_g, wuq, qn_g, kvn_g, wqi, kidx_g,
            vg, wsp, bsp_t, wpa):
    t, d = x2d.shape
    tm = min(TM_INPROJ, t)
    nqb = t // Q_BLOCK
    weights = (g1, wuv, wcq, wmisc, wga, wgb, qlat_g, wuq, qn_g, kvn_g, wqi, kidx_g,
               vg, wsp, bsp_t, wpa)
    out_shape = (
        jax.ShapeDtypeStruct((t, d), BF16),
        jax.ShapeDtypeStruct((t, d), BF16),
        jax.ShapeDtypeStruct((nqb, N_HEADS, Q_BLOCK, KV_RANK), BF16),
        jax.ShapeDtypeStruct((nqb, IDX_HEADS, Q_BLOCK, LANES), BF16),
        jax.ShapeDtypeStruct((t, 2 * KV_RANK), BF16),
        jax.ShapeDtypeStruct((t, LANES), BF16),
        jax.ShapeDtypeStruct((t, LANES), F32),
    )
    row = lambda w: pl.BlockSpec((tm, w), lambda i: (i, 0))
    out_specs = (
        row(d), row(d),
        pl.BlockSpec((tm // Q_BLOCK, N_HEADS, Q_BLOCK, KV_RANK), lambda i: (i, 0, 0, 0)),
        pl.BlockSpec((tm // Q_BLOCK, IDX_HEADS, Q_BLOCK, LANES), lambda i: (i, 0, 0, 0)),
        row(2 * KV_RANK), row(LANES), row(LANES),
    )
    return pl.pallas_call(
        _inproj_kernel,
        grid=(t // tm,),
        in_specs=[row(d)] + [_full_spec(w) for w in weights],
        out_specs=out_specs,
        out_shape=out_shape,
        scratch_shapes=[pltpu.VMEM((tm, GMLP_WIDTH), BF16)],
        compiler_params=pltpu.CompilerParams(
            dimension_semantics=("arbitrary",), vmem_limit_bytes=VMEM_LIMIT_BYTES),
        name="inproj",
    )(x2d, *weights)


def _dsa_kernel(q_ref, qi_ref, wi_ref, kv_ref, ki_ref, wuvp_ref, yb_ref,
                keys_ref, m_ref, alpha_ref, acc_ref, p_ref, *, kb, k_top):
    qb = pl.program_id(1)
    q0 = qb * Q_BLOCK
    nkb = (q0 + Q_BLOCK + kb - 1) // kb
    nt = (((1,), (1,)), ((), ()))

    qidx = qi_ref[0].reshape(IDX_HEADS * Q_BLOCK, LANES)
    wslab = wi_ref[...]
    w_cols = [wslab[:, IDX_DIM + h:IDX_DIM + h + 1] for h in range(IDX_HEADS)]
    row_t = q0 + lax.broadcasted_iota(I32, (Q_BLOCK, 1), 0)

    def score_block(j, carry):
        kblk = ki_ref[0, pl.ds(pl.multiple_of(j * kb, kb), kb), :]
        lg = lax.dot_general(qidx, kblk, nt, preferred_element_type=F32)
        sc = jnp.zeros((Q_BLOCK, kb), F32)
        for h in range(IDX_HEADS):
            sc = sc + jnp.maximum(lg[h * Q_BLOCK:(h + 1) * Q_BLOCK], 0.0) * w_cols[h]
        sc = jnp.where(sc == 0.0, 0.0, sc)
        bits = pltpu.bitcast(sc, I32)
        key = bits ^ ((bits >> 31) & 0x7FFFFFFF)
        kpos = j * kb + lax.broadcasted_iota(I32, (1, kb), 1)
        keys_ref[j] = jnp.where(kpos <= row_t, key, INT_MIN)
        return carry

    lax.fori_loop(0, nkb, score_block, 0)

    def count(cmp, thr):
        thr_b = jnp.broadcast_to(thr, (Q_BLOCK, LANES))

        def body(j, acc):
            k = keys_ref[j]
            for c in range(kb // LANES):
                acc = acc + jnp.where(cmp(k[:, c * LANES:(c + 1) * LANES], thr_b), 1.0, 0.0)
            return acc

        acc = lax.fori_loop(0, nkb, body, jnp.zeros((Q_BLOCK, LANES), F32))
        return jnp.sum(acc, axis=1, keepdims=True)

    ge = lambda a, b: a >= b
    gt = lambda a, b: a > b

    def bit_step(i, thr):
        cand = thr + lax.shift_left(jnp.int32(1), 31 - i)
        return jnp.where(count(ge, cand) >= k_top, cand, thr)

    thr = lax.fori_loop(0, 32, bit_step, jnp.full((Q_BLOCK, 1), INT_MIN, I32))
    thr = jnp.maximum(thr, INT_MIN + 1)
    has_ties = jnp.max(count(ge, thr)) > k_top
    need = lax.cond(has_ties, lambda: k_top - count(gt, thr),
                    lambda: jnp.zeros((Q_BLOCK, 1), F32))
    thr_kb = jnp.broadcast_to(thr, (Q_BLOCK, kb))

    def bias_ties(k, run):
        eq = k == thr_kb
        r_io = lax.broadcasted_iota(I32, (kb, kb), 0)
        c_io = lax.broadcasted_iota(I32, (kb, kb), 1)
        upper = jnp.where(r_io <= c_io, 1.0, 0.0).astype(BF16)
        pref = jnp.dot(jnp.where(eq, 1.0, 0.0).astype(BF16), upper,
                       preferred_element_type=F32)
        sel = (k > thr_kb) | (eq & (pref + run <= need))
        return jnp.where(sel, 0.0, NEG_BIG), run + pref[:, kb - 1:kb]

    def bias_plain(k, run):
        return jnp.where(k >= thr_kb, 0.0, NEG_BIG), run

    m_ref[...] = jnp.full(m_ref.shape, NEG_BIG, F32)
    acc_ref[...] = jnp.zeros(acc_ref.shape, F32)
    grp = HEAD_GROUP * Q_BLOCK

    def attn_block(j, run):
        kva = kv_ref[0, pl.ds(pl.multiple_of(j * kb, kb), kb), :]
        kvb = kva[:, :KV_RANK]
        bias, run = lax.cond(has_ties, bias_ties, bias_plain, keys_ref[j], run)
        for g in range(N_HEADS // HEAD_GROUP):
            g_rows = pl.ds(g * grp, grp)
            qg = q_ref[0, g * HEAD_GROUP:(g + 1) * HEAD_GROUP].reshape(grp, KV_RANK)
            logits = lax.dot_general(qg, kvb, nt, preferred_element_type=F32)
            for hh in range(HEAD_GROUP):
                rows = pl.ds(g * grp + hh * Q_BLOCK, Q_BLOCK)
                s = logits[hh * Q_BLOCK:(hh + 1) * Q_BLOCK] + bias
                m_old = m_ref[rows]
                m_new = jnp.maximum(m_old, jnp.max(s, axis=1, keepdims=True))
                alpha_ref[rows] = jnp.exp2(m_old - m_new)
                m_ref[rows] = m_new
                for c in range(kb // LANES):
                    p_ref[rows, pl.ds(c * LANES, LANES)] = jnp.exp2(
                        s[:, c * LANES:(c + 1) * LANES] - m_new).astype(BF16)
            pv = jnp.dot(p_ref[g_rows], kva, preferred_element_type=F32)
            a = alpha_ref[g_rows]
            lo, hi = pl.ds(0, KV_RANK), pl.ds(KV_RANK, KV_RANK)
            acc_ref[g_rows, lo] = a * acc_ref[g_rows, lo] + pv[:, :KV_RANK]
            acc_ref[g_rows, hi] = a * acc_ref[g_rows, hi] + pv[:, KV_RANK:]
        return run

    lax.fori_loop(0, nkb, attn_block, jnp.zeros((Q_BLOCK, 1), F32))

    o = (acc_ref[:, pl.ds(0, KV_RANK)] / acc_ref[:, pl.ds(KV_RANK, 1)]).astype(BF16)
    y = jnp.zeros((Q_BLOCK, ATTN_WIDTH), F32)
    for h in range(N_HEADS):
        y = y + jnp.dot(o[h * Q_BLOCK:(h + 1) * Q_BLOCK], wuvp_ref[h],
                        preferred_element_type=F32)
    yb_ref[...] = y.astype(BF16)


def _dsa(q_hm, qi_hm, wi, kv, ki, wuvp, batch, seq):
    nq = seq // Q_BLOCK
    kb = min(KEY_BLOCK, seq)
    k_top = min(TOPK_MAX, seq // 4)
    rows = N_HEADS * Q_BLOCK
    kern = functools.partial(_dsa_kernel, kb=kb, k_top=float(k_top))
    return pl.pallas_call(
        kern,
        grid=(batch, nq),
        in_specs=[
            pl.BlockSpec((1, N_HEADS, Q_BLOCK, KV_RANK), lambda b, i: (b * nq + i, 0, 0, 0)),
            pl.BlockSpec((1, IDX_HEADS, Q_BLOCK, LANES), lambda b, i: (b * nq + i, 0, 0, 0)),
            pl.BlockSpec((Q_BLOCK, LANES), lambda b, i: (b * nq + i, 0)),
            pl.BlockSpec((1, seq, 2 * KV_RANK), lambda b, i: (b, 0, 0)),
            pl.BlockSpec((1, seq, LANES), lambda b, i: (b, 0, 0)),
            _full_spec(wuvp),
        ],
        out_specs=pl.BlockSpec((Q_BLOCK, ATTN_WIDTH), lambda b, i: (b * nq + i, 0)),
        out_shape=jax.ShapeDtypeStruct((batch * seq, ATTN_WIDTH), BF16),
        scratch_shapes=[
            pltpu.VMEM((seq // kb, Q_BLOCK, kb), I32),
            pltpu.VMEM((rows, LANES), F32),
            pltpu.VMEM((rows, LANES), F32),
            pltpu.VMEM((rows, 2 * KV_RANK), F32),
            pltpu.VMEM((rows, kb), BF16),
        ],
        compiler_params=pltpu.CompilerParams(
            dimension_semantics=("arbitrary", "arbitrary"), vmem_limit_bytes=VMEM_LIMIT_BYTES),
        name="dsa",
    )(q_hm, qi_hm, wi, kv.reshape(batch, seq, 2 * KV_RANK), ki.reshape(batch, seq, LANES), wuvp)


def _merge_kernel(x_ref, pa_ref, gb_ref, yb_ref, wpb_ref, wout_ref, g2_ref, wr_ref, br_ref,
                  x2_ref, xn2_ref, te_ref, gate_ref, rank_ref, cnt_ref, carry_ref):
    tm = x_ref.shape[0]

    @pl.when(pl.program_id(0) == 0)
    def _():
        carry_ref[...] = jnp.zeros(carry_ref.shape, F32)

    pb = jnp.dot(yb_ref[...], wpb_ref[...], preferred_element_type=F32)
    merged = pa_ref[...].astype(F32) + gb_ref[...].astype(F32) * pb
    x2 = x_ref[...] + jnp.dot(merged.astype(BF16), wout_ref[...], preferred_element_type=F32)
    x2_ref[...] = x2
    xn2 = _rms(x2) * g2_ref[...]
    xn2_ref[...] = xn2

    logits = jnp.dot(xn2, wr_ref[...], preferred_element_type=F32,
                     precision=lax.Precision.HIGHEST) + br_ref[...]
    lane = lax.broadcasted_iota(I32, (tm, LANES), 1).astype(F32)
    work = jnp.where(lane < N_EXPERTS, logits, -jnp.inf)
    onehot = jnp.zeros((tm, LANES), F32)
    vals, idxs = [], []
    for _ in range(TOP_K):
        mx = jnp.max(work, axis=1, keepdims=True)
        ix = jnp.min(jnp.where(work == mx, lane, float(LANES)), axis=1, keepdims=True)
        hit = lane == ix
        onehot = onehot + jnp.where(hit, 1.0, 0.0)
        work = jnp.where(hit, -jnp.inf, work)
        vals.append(mx)
        idxs.append(ix)
    exps = [jnp.exp(v - vals[0]) for v in vals]
    denom = exps[0] + exps[1] + exps[2] + exps[3]

    r_io = lax.broadcasted_iota(I32, (tm, tm), 0)
    c_io = lax.broadcasted_iota(I32, (tm, tm), 1)
    lower = jnp.where(c_io <= r_io, 1.0, 0.0).astype(BF16)
    pref = jnp.dot(lower, onehot.astype(BF16), preferred_element_type=F32)
    base = pref - onehot + carry_ref[...]
    te = jnp.zeros((tm, LANES), F32)
    gate = jnp.zeros((tm, LANES), F32)
    rank = jnp.zeros((tm, LANES), F32)
    for k in range(TOP_K):
        slot = lane == float(k)
        rk = jnp.sum(jnp.where(lane == idxs[k], base, 0.0), axis=1, keepdims=True)
        te = jnp.where(slot, idxs[k], te)
        gate = jnp.where(slot, exps[k] / denom, gate)
        rank = jnp.where(slot, rk, rank)
    te_ref[...] = te.astype(I32)
    gate_ref[...] = gate
    rank_ref[...] = rank.astype(I32)
    carry_ref[...] = carry_ref[...] + pref[tm - 1:tm, :]
    cnt_ref[...] = carry_ref[...]


def _merge(x2d, pa, gb, yb, wpb, wout, g2, wr, br):
    t, d = x2d.shape
    tm = min(TM_MERGE, t)
    row = lambda w: pl.BlockSpec((tm, w), lambda i: (i, 0))
    weights = (wpb, wout, g2, wr, br)
    out_shape = (
        jax.ShapeDtypeStruct((t, d), F32),
        jax.ShapeDtypeStruct((t, d), F32),
        jax.ShapeDtypeStruct((t, LANES), I32),
        jax.ShapeDtypeStruct((t, LANES), F32),
        jax.ShapeDtypeStruct((t, LANES), I32),
        jax.ShapeDtypeStruct((1, LANES), F32),
    )
    return pl.pallas_call(
        _merge_kernel,
        grid=(t // tm,),
        in_specs=[row(d), row(d), row(d), row(ATTN_WIDTH)] + [_full_spec(w) for w in weights],
        out_specs=(row(d), row(d), row(LANES), row(LANES), row(LANES),
                   pl.BlockSpec((1, LANES), lambda i: (0, 0))),
        out_shape=out_shape,
        scratch_shapes=[pltpu.VMEM((1, LANES), F32)],
        compiler_params=pltpu.CompilerParams(
            dimension_semantics=("arbitrary",), vmem_limit_bytes=VMEM_LIMIT_BYTES),
        name="merge",
    )(x2d, pa, gb, yb, *weights)


def _row_copy(src, src_row, dst, dst_row, sem):
    return pltpu.make_async_copy(src.at[pl.ds(src_row, 1)], dst.at[pl.ds(dst_row, 1)], sem)


def _dispatch_kernel(pstart_ref, cnt_ref, nused_ref, te_ref, rank_ref, xn2_ref, xs_ref,
                     zero_ref, sem, zsem):
    tm = xn2_ref.shape[0]
    n_blocks = xs_ref.shape[0] // MOE_BLOCK

    @pl.when(pl.program_id(0) == 0)
    def _():
        zero_ref[...] = jnp.zeros(zero_ref.shape, F32)

        def pad_expert(e, carry):
            cnt = cnt_ref[e]
            n_pad = (cnt + MOE_BLOCK - 1) // MOE_BLOCK * MOE_BLOCK - cnt
            first = pstart_ref[e] + cnt

            def start(r, c):
                _row_copy(zero_ref, 0, xs_ref, first + r, zsem).start()
                return c

            def wait(r, c):
                _row_copy(zero_ref, 0, xs_ref, 0, zsem).wait()
                return c

            lax.fori_loop(0, n_pad, start, 0)
            lax.fori_loop(0, n_pad, wait, 0)
            return carry

        lax.fori_loop(0, N_EXPERTS, pad_expert, 0)

        def zero_block(b, carry):
            cp = pltpu.make_async_copy(
                zero_ref, xs_ref.at[pl.ds(pl.multiple_of(b * MOE_BLOCK, MOE_BLOCK), MOE_BLOCK)], zsem)
            cp.start()
            cp.wait()
            return carry

        lax.fori_loop(nused_ref[0], n_blocks, zero_block, 0)

    def issue(t, carry):
        for k in range(TOP_K):
            dest = pstart_ref[te_ref[t * TOP_K + k]] + rank_ref[t * TOP_K + k]
            _row_copy(xn2_ref, t, xs_ref, dest, sem).start()
        return carry

    lax.fori_loop(0, tm, issue, 0)

    def drain(t, carry):
        for k in range(TOP_K):
            _row_copy(xn2_ref, 0, xs_ref, 0, sem).wait()
        return carry

    lax.fori_loop(0, tm, drain, 0)


def _dispatch(pstart, counts, n_used, te_flat, rank_flat, xn2, n_rows):
    t, d = xn2.shape
    tm = min(TM_ROWS, t)
    smem_blk = pl.BlockSpec((tm * TOP_K,), lambda i, *_: (i,), memory_space=pltpu.SMEM)
    grid_spec = pltpu.PrefetchScalarGridSpec(
        num_scalar_prefetch=3,
        grid=(t // tm,),
        in_specs=[smem_blk, smem_blk, pl.BlockSpec((tm, d), lambda i, *_: (i, 0))],
        out_specs=pl.BlockSpec(memory_space=pl.ANY),
        scratch_shapes=[pltpu.VMEM((MOE_BLOCK, d), F32),
                        pltpu.SemaphoreType.DMA(()), pltpu.SemaphoreType.DMA(())],
    )
    return pl.pallas_call(
        _dispatch_kernel,
        grid_spec=grid_spec,
        out_shape=jax.ShapeDtypeStruct((n_rows, d), F32),
        compiler_params=pltpu.CompilerParams(dimension_semantics=("arbitrary",)),
        name="dispatch",
    )(pstart, counts, n_used, te_flat, rank_flat, xn2)


def _moe_kernel(blk_e_ref, xs_ref, w1_ref, b1_ref, w2_ref, b2_ref, y_ref, w1b_ref, w2b_ref):
    i = pl.program_id(0)
    prev_e = blk_e_ref[jnp.maximum(i - 1, 0)]

    @pl.when(jnp.logical_or(i == 0, blk_e_ref[i] != prev_e))
    def _():
        def cast(src_ref, dst_ref):
            def body(r, c):
                rows = pl.ds(pl.multiple_of(r * LANES, LANES), LANES)
                dst_ref[rows, :] = src_ref[0, rows, :].astype(BF16)
                return c
            lax.fori_loop(0, src_ref.shape[1] // LANES, body, 0)

        cast(w1_ref, w1b_ref)
        cast(w2_ref, w2b_ref)

    xb = xs_ref[...].astype(BF16)
    hb = jnp.dot(xb, w1b_ref[...], preferred_element_type=F32) + b1_ref[0]
    hg = jnp.minimum(hb[:, :D_EXPERT], SWIGLU_LIMIT)
    hu = jnp.clip(hb[:, D_EXPERT:], -SWIGLU_LIMIT, SWIGLU_LIMIT)
    act = (hu + 1.0) * (hg * jax.nn.sigmoid(SWIGLU_ALPHA * hg))
    y_ref[...] = jnp.dot(act.astype(BF16), w2b_ref[...], preferred_element_type=F32) + b2_ref[0]


def _moe(blk_e, xs, w1, b1, w2, b2):
    n_rows, d = xs.shape
    n_blocks = n_rows // MOE_BLOCK
    grid_spec = pltpu.PrefetchScalarGridSpec(
        num_scalar_prefetch=1,
        grid=(n_blocks,),
        in_specs=[
            pl.BlockSpec((MOE_BLOCK, d), lambda i, be: (i, 0)),
            pl.BlockSpec((1, d, 2 * D_EXPERT), lambda i, be: (be[i], 0, 0)),
            pl.BlockSpec((1, 1, 2 * D_EXPERT), lambda i, be: (be[i], 0, 0)),
            pl.BlockSpec((1, D_EXPERT, d), lambda i, be: (be[i], 0, 0)),
            pl.BlockSpec((1, 1, d), lambda i, be: (be[i], 0, 0)),
        ],
        out_specs=pl.BlockSpec((MOE_BLOCK, d), lambda i, be: (i, 0)),
        scratch_shapes=[pltpu.VMEM((d, 2 * D_EXPERT), BF16), pltpu.VMEM((D_EXPERT, d), BF16)],
    )
    return pl.pallas_call(
        _moe_kernel,
        grid_spec=grid_spec,
        out_shape=jax.ShapeDtypeStruct((n_rows, d), F32),
        compiler_params=pltpu.CompilerParams(
            dimension_semantics=("arbitrary",), vmem_limit_bytes=VMEM_LIMIT_BYTES),
        name="moe",
    )(blk_e, xs, w1, b1, w2, b2)


def _combine_kernel(pstart_ref, te_ref, rank_ref, x2_ref, gate_ref, ys_ref, out_ref, buf_ref, sem):
    tm = x2_ref.shape[0]

    def issue(t, carry):
        for k in range(TOP_K):
            src = pstart_ref[te_ref[t * TOP_K + k]] + rank_ref[t * TOP_K + k]
            _row_copy(ys_ref, src, buf_ref.at[k], t, sem).start()
        return carry

    lax.fori_loop(0, tm, issue, 0)

    def drain(t, carry):
        for k in range(TOP_K):
            _row_copy(ys_ref, 0, buf_ref.at[k], 0, sem).wait()
        return carry

    lax.fori_loop(0, tm, drain, 0)

    gate = gate_ref[...]
    out = x2_ref[...]
    for k in range(TOP_K):
        out = out + gate[:, k:k + 1] * buf_ref[k]
    out_ref[...] = out


def _combine(pstart, te_flat, rank_flat, x2, gate, ys):
    t, d = x2.shape
    tm = min(TM_ROWS, t)
    smem_blk = pl.BlockSpec((tm * TOP_K,), lambda i, ps: (i,), memory_space=pltpu.SMEM)
    grid_spec = pltpu.PrefetchScalarGridSpec(
        num_scalar_prefetch=1,
        grid=(t // tm,),
        in_specs=[smem_blk, smem_blk,
                  pl.BlockSpec((tm, d), lambda i, ps: (i, 0)),
                  pl.BlockSpec((tm, LANES), lambda i, ps: (i, 0)),
                  pl.BlockSpec(memory_space=pl.ANY)],
        out_specs=pl.BlockSpec((tm, d), lambda i, ps: (i, 0)),
        scratch_shapes=[pltpu.VMEM((TOP_K, tm, d), F32), pltpu.SemaphoreType.DMA(())],
    )
    return pl.pallas_call(
        _combine_kernel,
        grid_spec=grid_spec,
        out_shape=jax.ShapeDtypeStruct((t, d), F32),
        compiler_params=pltpu.CompilerParams(
            dimension_semantics=("arbitrary",), vmem_limit_bytes=VMEM_LIMIT_BYTES),
        name="combine",
    )(pstart, te_flat, rank_flat, x2, gate, ys)


def kernel(x, norm1_g, w_in, q_lat_g, w_uq, q_norm_g, kv_norm_g, w_uv, w_q_idx, k_idx_g,
           gmlp_v_g, w_spatial, b_spatial, w_proj_a, w_proj_b, w_out, norm2_g,
           w_router, b_router, w_exp1, b_exp1, w_exp2, b_exp2):
    batch, seq, d = x.shape
    t = batch * seq
    assert d == D_MODEL and seq % min(KEY_BLOCK, seq) == 0 and seq % Q_BLOCK == 0
    assert t % TM_INPROJ == 0 and t % TM_MERGE == 0 and (t * TOP_K) % MOE_BLOCK == 0
    x2d = x.reshape(t, d)
    row = lambda v: v.reshape(1, -1).astype(F32)

    wuv = w_in[:, OFF_UV:OFF_CQ].astype(BF16)
    wcq = w_in[:, OFF_CQ:OFF_CKV].astype(BF16)
    wmisc = jnp.pad(w_in[:, OFF_CKV:OFF_GATE], ((0, 0), (0, 2 * LANES - (OFF_GATE - OFF_CKV)))
                    ).astype(BF16)
    wga = w_in[:, OFF_GATE:OFF_GATE + D_MODEL].astype(BF16)
    wgb = w_in[:, OFF_GATE + D_MODEL:IN_WIDTH].astype(BF16)
    wuq = w_uq.reshape(Q_RANK, N_HEADS * KV_RANK).astype(BF16)
    wqi = jnp.pad(w_q_idx * (IDX_DIM ** -0.5), ((0, 0), (0, 0), (0, LANES - IDX_DIM))
                  ).reshape(Q_RANK, IDX_HEADS * LANES).astype(BF16)
    kidx_g = jnp.pad(k_idx_g, (0, LANES - IDX_DIM)).reshape(1, LANES).astype(F32)
    qn_g = row(q_norm_g) * (KV_RANK ** -0.5 * LOG2E)
    wuvp = jnp.zeros((N_HEADS, KV_RANK, ATTN_WIDTH), F32)
    for h in range(N_HEADS):
        wuvp = wuvp.at[h, :, h * HEAD_V:(h + 1) * HEAD_V].set(w_uv[h])
    wuvp = wuvp.astype(BF16)

    pa, gb, q_hm, qi_hm, kv, ki, wi = _inproj(
        x2d, row(norm1_g), wuv, wcq, wmisc, wga, wgb, row(q_lat_g), wuq, qn_g,
        row(kv_norm_g), wqi, kidx_g, row(gmlp_v_g), w_spatial.astype(F32),
        b_spatial.T.astype(F32), w_proj_a.astype(BF16))

    yb = _dsa(q_hm, qi_hm, wi, kv, ki, wuvp, batch, seq)

    wr = jnp.pad(w_router, ((0, 0), (0, LANES - N_EXPERTS))).astype(F32)
    br = jnp.pad(b_router, (0, LANES - N_EXPERTS)).reshape(1, LANES).astype(F32)
    x2, xn2, te, gate, rank, cnt = _merge(
        x2d, pa, gb, yb, w_proj_b.astype(BF16), w_out.astype(BF16), row(norm2_g), wr, br)

    counts = cnt[0, :N_EXPERTS].astype(I32)
    padded = (counts + MOE_BLOCK - 1) // MOE_BLOCK * MOE_BLOCK
    pad_end = jnp.cumsum(padded)
    pstart = (pad_end - padded).astype(I32)
    n_blocks = (t * TOP_K) // MOE_BLOCK + N_EXPERTS
    blk_start = jnp.arange(n_blocks, dtype=I32) * MOE_BLOCK
    blk_e = jnp.minimum(jnp.sum((pad_end[None, :] <= blk_start[:, None]).astype(I32), axis=1),
                        N_EXPERTS - 1).astype(I32)
    n_used = (pad_end[N_EXPERTS - 1:] // MOE_BLOCK).astype(I32)
    te_flat = te[:, :TOP_K].reshape(t * TOP_K)
    rank_flat = rank[:, :TOP_K].reshape(t * TOP_K)

    xs = _dispatch(pstart, counts, n_used, te_flat, rank_flat, xn2, n_blocks * MOE_BLOCK)
    ys = _moe(blk_e, xs, w_exp1.astype(F32), b_exp1.reshape(N_EXPERTS, 1, -1).astype(F32),
              w_exp2.astype(F32), b_exp2.reshape(N_EXPERTS, 1, -1).astype(F32))
    out = _combine(pstart, te_flat, rank_flat, x2, gate, ys)
    return out.reshape(batch, seq, d)
```

```python
import functools
import math

import jax
import jax.numpy as jnp
from jax import lax
from jax.experimental import pallas as pl
from jax.experimental.pallas import tpu as pltpu

F32 = jnp.float32
BF16 = jnp.bfloat16
I32 = jnp.int32

D_MODEL = 1024
GMLP_GROUPS = 4
GMLP_GROUP_DIM = 128
GMLP_WIDTH = GMLP_GROUPS * GMLP_GROUP_DIM
CHUNK = 128
N_HEADS = 8
Q_RANK = 256
KV_RANK = 128
HEAD_V = 64
ATTN_WIDTH = N_HEADS * HEAD_V
IDX_HEADS = 4
IDX_DIM = 64
TOPK_MAX = 256
N_EXPERTS = 32
TOP_K = 4
D_EXPERT = 1024
SWIGLU_LIMIT = 7.0
SWIGLU_ALPHA = 1.702
EPS = 1e-6

OFF_UV = 0
OFF_CQ = OFF_UV + 2 * GMLP_WIDTH
OFF_CKV = OFF_CQ + Q_RANK
OFF_KIDX = OFF_CKV + KV_RANK
OFF_WIDX = OFF_KIDX + IDX_DIM
OFF_GATE = OFF_WIDX + IDX_HEADS
IN_WIDTH = OFF_GATE + 2 * D_MODEL

LANES = 128
Q_BLOCK = 256
VMEM_LIMIT_BYTES = 56 * 1024 * 1024

TM_INPROJ = 512
KEY_BLOCK = 512
COUNT_ROWS = 128
SOFTMAX_ROWS = 64
TM_MERGE = 512
TM_ROWS = 256
MOE_BLOCK = 256

I16 = jnp.int16
INT_MIN = -(2 ** 31)
HALF16 = 2 ** 15
NEG_BIG = -1e30
LOG2E = 1.4426950408889634


def _full_spec(arr):
    nd = arr.ndim
    return pl.BlockSpec(arr.shape, lambda *_: (0,) * nd)


def _rms(v, axis=-1):
    return v * lax.rsqrt(jnp.mean(v * v, axis=axis, keepdims=True) + EPS)


def _gelu_tanh(v):
    c = math.sqrt(2.0 / math.pi)
    return 0.5 * v * (1.0 + jnp.tanh(c * (v + 0.044715 * (v * v * v))))


def _inproj_kernel(x_ref, g1_ref, wuv_ref, wcq_ref, wmisc_ref, wga_ref, wgb_ref,
                   qlat_g_ref, wuq_ref, qn_g_ref, kvn_g_ref, wqi_ref, kidx_g_ref,
                   vg_ref, wsp_ref, bsp_ref, wpa_ref,
                   pa_ref, gb_ref, q_ref, qi_ref, kv_ref, ki_ref, wi_ref,
                   ya_ref):
    tm = x_ref.shape[0]
    x = x_ref[...]
    xn = (_rms(x) * g1_ref[...]).astype(BF16)

    z = _gelu_tanh(jnp.dot(xn, wuv_ref[...], preferred_element_type=F32))
    r_io = lax.broadcasted_iota(I32, (CHUNK, CHUNK), 0)
    c_io = lax.broadcasted_iota(I32, (CHUNK, CHUNK), 1)
    causal = c_io <= r_io
    for g in range(GMLP_GROUPS):
        cols = slice(g * GMLP_GROUP_DIM, (g + 1) * GMLP_GROUP_DIM)
        u_g = z[:, cols]
        v_g = z[:, GMLP_WIDTH + g * GMLP_GROUP_DIM:GMLP_WIDTH + (g + 1) * GMLP_GROUP_DIM]
        vn_g = (_rms(v_g) * vg_ref[...]).astype(BF16)
        w_g = jnp.where(causal, wsp_ref[g], 0.0).astype(BF16)
        b_g = bsp_ref[:, g:g + 1]
        for c in range(tm // CHUNK):
            rows = slice(c * CHUNK, (c + 1) * CHUNK)
            sv = jnp.dot(w_g, vn_g[rows], preferred_element_type=F32) + b_g
            ya_ref[rows, cols] = (u_g[rows] * sv).astype(BF16)
    gate_a = jax.nn.sigmoid(jnp.dot(xn, wga_ref[...], preferred_element_type=F32))
    pa = gate_a * jnp.dot(ya_ref[...], wpa_ref[...], preferred_element_type=F32)
    pa_ref[...] = pa.astype(BF16)
    gb_ref[...] = jax.nn.sigmoid(
        jnp.dot(xn, wgb_ref[...], preferred_element_type=F32)).astype(BF16)

    c_q = (_rms(jnp.dot(xn, wcq_ref[...], preferred_element_type=F32))
           * qlat_g_ref[...]).astype(BF16)
    qf = jnp.dot(c_q, wuq_ref[...], preferred_element_type=F32)
    qif = jnp.dot(c_q, wqi_ref[...], preferred_element_type=F32)
    for h in range(N_HEADS):
        qh = (_rms(qf[:, h * KV_RANK:(h + 1) * KV_RANK]) * qn_g_ref[...]).astype(BF16)
        for j in range(tm // Q_BLOCK):
            q_ref[j, h] = qh[j * Q_BLOCK:(j + 1) * Q_BLOCK]
    for h in range(IDX_HEADS):
        qih = qif[:, h * LANES:(h + 1) * LANES].astype(BF16)
        for j in range(tm // Q_BLOCK):
            qi_ref[j, h] = qih[j * Q_BLOCK:(j + 1) * Q_BLOCK]

    hm = jnp.dot(xn, wmisc_ref[...], preferred_element_type=F32)
    hk = hm[:, KV_RANK:]
    lane = lax.broadcasted_iota(I32, hk.shape, 1)
    kv_ref[:, :KV_RANK] = (_rms(hm[:, :KV_RANK]) * kvn_g_ref[...]).astype(BF16)
    kv_ref[:, KV_RANK:] = jnp.where(lane == 0, 1.0, 0.0).astype(BF16)
    is_k = lane < IDX_DIM
    ms_k = jnp.sum(jnp.where(is_k, hk * hk, 0.0), axis=-1, keepdims=True) * (1.0 / IDX_DIM)
    ki = jnp.where(is_k, hk * lax.rsqrt(ms_k + EPS) * kidx_g_ref[...], 0.0)
    ki_ref[...] = ki.astype(BF16)
    wi_ref[...] = hk * (IDX_HEADS ** -0.5)


def _inproj(x2d, g1, wuv, wcq, wmisc, wga, wgb, qlat_g, wuq, qn_g, kvn_g, wqi, kidx_g,
            vg, wsp, bsp_t, wpa):
    t, d = x2d.shape
    tm = min(TM_INPROJ, t)
    nqb = t // Q_BLOCK
    weights = (g1, wuv, wcq, wmisc, wga, wgb, qlat_g, wuq, qn_g, kvn_g, wqi, kidx_g,
               vg, wsp, bsp_t, wpa)
    out_shape = (
        jax.ShapeDtypeStruct((t, d), BF16),
        jax.ShapeDtypeStruct((t, d), BF16),
        jax.ShapeDtypeStruct((nqb, N_HEADS, Q_BLOCK, KV_RANK), BF16),
        jax.ShapeDtypeStruct((nqb, IDX_HEADS, Q_BLOCK, LANES), BF16),
        jax.ShapeDtypeStruct((t, 2 * KV_RANK), BF16),
        jax.ShapeDtypeStruct((t, LANES), BF16),
        jax.ShapeDtypeStruct((t, LANES), F32),
    )
    row = lambda w: pl.BlockSpec((tm, w), lambda i: (i, 0))
    out_specs = (
        row(d), row(d),
        pl.BlockSpec((tm // Q_BLOCK, N_HEADS, Q_BLOCK, KV_RANK), lambda i: (i, 0, 0, 0)),
        pl.BlockSpec((tm // Q_BLOCK, IDX_HEADS, Q_BLOCK, LANES), lambda i: (i, 0, 0, 0)),
        row(2 * KV_RANK), row(LANES), row(LANES),
    )
    return pl.pallas_call(
        _inproj_kernel,
        grid=(t // tm,),
        in_specs=[row(d)] + [_full_spec(w) for w in weights],
        out_specs=out_specs,
        out_shape=out_shape,
        scratch_shapes=[pltpu.VMEM((tm, GMLP_WIDTH), BF16)],
        compiler_params=pltpu.CompilerParams(
            dimension_semantics=("arbitrary",), vmem_limit_bytes=VMEM_LIMIT_BYTES),
        name="inproj",
    )(x2d, *weights)


def _dsa_kernel(q_ref, qi_ref, wi_ref, kv_ref, ki_ref, wuvp_ref, yb_ref,
                keys_ref, m_ref, alpha_ref, acc_ref, p_ref, bias_ref, s0_ref, s1_ref,
                thr_ref, need_ref, run_ref, hi_ref, lo_ref, mlo_ref,
                *, kb, k_top):
    qb = pl.program_id(1)
    q0 = qb * Q_BLOCK
    nkb = (q0 + Q_BLOCK + kb - 1) // kb
    nt = (((1,), (1,)), ((), ()))

    qidx = qi_ref[0].reshape(IDX_HEADS * Q_BLOCK, LANES)
    wslab = wi_ref[...]
    w_cols = [wslab[:, IDX_DIM + h:IDX_DIM + h + 1] for h in range(IDX_HEADS)]
    row_t = q0 + lax.broadcasted_iota(I32, (Q_BLOCK, 1), 0)

    def score_block(j, carry):
        kblk = ki_ref[0, pl.ds(pl.multiple_of(j * kb, kb), kb), :]
        lg = lax.dot_general(qidx, kblk, nt, preferred_element_type=F32)
        sc = jnp.zeros((Q_BLOCK, kb), F32)
        for h in range(IDX_HEADS):
            sc = sc + jnp.maximum(lg[h * Q_BLOCK:(h + 1) * Q_BLOCK], 0.0) * w_cols[h]
        sc = jnp.where(sc == 0.0, 0.0, sc)
        bits = pltpu.bitcast(sc, I32)
        key = bits ^ ((bits >> 31) & 0x7FFFFFFF)
        kpos = j * kb + lax.broadcasted_iota(I32, (1, kb), 1)
        key = jnp.where(kpos <= row_t, key, INT_MIN)
        keys_ref[j] = key
        hi_ref[j] = (key >> 16).astype(I16)
        lo_ref[j] = ((key & 0xFFFF) - HALF16).astype(I16)
        return carry

    lax.fori_loop(0, nkb, score_block, 0)

    ones = jnp.ones((LANES, LANES), BF16)

    def count(cmp, thr):
        partial = []
        for r in range(Q_BLOCK // COUNT_ROWS):
            rows = pl.ds(r * COUNT_ROWS, COUNT_ROWS)
            thr_b = thr[r * COUNT_ROWS:(r + 1) * COUNT_ROWS]

            def body(j, acc, rows=rows, thr_b=thr_b):
                for c in range(kb // LANES):
                    k = keys_ref[j, rows, pl.ds(c * LANES, LANES)]
                    acc = acc + jnp.where(cmp(k, thr_b), 1.0, 0.0)
                return acc

            acc = lax.fori_loop(0, nkb, body, jnp.zeros((COUNT_ROWS, LANES), F32))
            partial.append(acc.astype(BF16))
        return jnp.dot(jnp.concatenate(partial, axis=0), ones, preferred_element_type=F32)

    ge = lambda a, b: a >= b
    gt = lambda a, b: a > b

    def count16(plane_ref, cmp, thr16):
        def body(j, acc):
            for c in range(kb // LANES):
                k = plane_ref[j, :, pl.ds(c * LANES, LANES)]
                acc = acc + jnp.where(cmp(k, thr16), jnp.int16(1), jnp.int16(0))
            return acc

        acc = lax.fori_loop(0, nkb, body, jnp.zeros((Q_BLOCK, LANES), I16))
        return jnp.dot(acc.astype(F32).astype(BF16), ones, preferred_element_type=F32)

    def search16(plane_ref, base):
        def step(i, t):
            cand = t + lax.shift_left(jnp.int32(1), 15 - i)
            ok = base + count16(plane_ref, ge, cand.astype(I16)) >= k_top
            return jnp.where(ok, cand, t)
        return lax.fori_loop(0, 16, step, jnp.full((Q_BLOCK, LANES), -HALF16, I32))

    t_hi = search16(hi_ref, 0.0)
    t_hi16 = t_hi.astype(I16)
    n_above = count16(hi_ref, gt, t_hi16)

    def member_block(j, carry):
        for r in range(Q_BLOCK // COUNT_ROWS):
            rows = pl.ds(r * COUNT_ROWS, COUNT_ROWS)
            t_b = t_hi16[r * COUNT_ROWS:(r + 1) * COUNT_ROWS]
            for c in range(kb // LANES):
                cols = pl.ds(c * LANES, LANES)
                mlo_ref[j, rows, cols] = jnp.where(hi_ref[j, rows, cols] == t_b,
                                                   lo_ref[j, rows, cols], jnp.int16(-HALF16))
        return carry

    lax.fori_loop(0, nkb, member_block, 0)
    t_lo = search16(mlo_ref, n_above)
    thr = lax.shift_left(t_hi, 16) + (t_lo + HALF16)
    thr = jnp.maximum(thr, INT_MIN + 1)
    has_ties = jnp.max(count(ge, thr)) > k_top
    thr_ref[...] = thr

    @pl.when(has_ties)
    def _():
        need_ref[...] = k_top - count(gt, thr)
        run_ref[...] = jnp.zeros(run_ref.shape, F32)

    def bias_ties(j):
        k = keys_ref[j]
        t = thr_ref[:, 0:1]
        eq = k == t
        r_io = lax.broadcasted_iota(I32, (kb, kb), 0)
        c_io = lax.broadcasted_iota(I32, (kb, kb), 1)
        upper = jnp.where(r_io <= c_io, 1.0, 0.0).astype(BF16)
        pref = jnp.dot(jnp.where(eq, 1.0, 0.0).astype(BF16), upper,
                       preferred_element_type=F32)
        run = run_ref[:, 0:1]
        sel = (k > t) | (eq & (pref + run <= need_ref[:, 0:1]))
        bias_ref[...] = jnp.where(sel, 0.0, NEG_BIG)
        run_ref[...] = jnp.broadcast_to(run + pref[:, kb - 1:kb], run_ref.shape)

    def bias_plain(j):
        for r in range(Q_BLOCK // SOFTMAX_ROWS):
            sub = pl.ds(r * SOFTMAX_ROWS, SOFTMAX_ROWS)
            t = thr_ref[sub]
            for c in range(kb // LANES):
                cols = pl.ds(c * LANES, LANES)
                bias_ref[sub, cols] = jnp.where(keys_ref[j, sub, cols] >= t, 0.0, NEG_BIG)

    m_ref[...] = jnp.full(m_ref.shape, NEG_BIG, F32)
    acc_ref[...] = jnp.zeros(acc_ref.shape, F32)
    s_refs = (s0_ref, s1_ref)
    lo, hi = pl.ds(0, KV_RANK), pl.ds(KV_RANK, KV_RANK)

    def attn_block(j, carry):
        keys_of_block = pl.ds(pl.multiple_of(j * kb, kb), kb)
        lax.cond(has_ties, bias_ties, bias_plain, j)
        for h in range(N_HEADS):
            s_ref = s_refs[h % 2]
            s_ref[...] = lax.dot_general(q_ref[0, h], kv_ref[0, keys_of_block, lo], nt,
                                         preferred_element_type=F32)
            for r in range(Q_BLOCK // SOFTMAX_ROWS):
                sub = pl.ds(r * SOFTMAX_ROWS, SOFTMAX_ROWS)
                rows = pl.ds(h * Q_BLOCK + r * SOFTMAX_ROWS, SOFTMAX_ROWS)
                s = s_ref[sub, :] + bias_ref[sub, :]
                s_ref[sub, :] = s
                m_old = m_ref[rows]
                m_new = jnp.maximum(m_old, jnp.max(s, axis=1, keepdims=True))
                alpha_ref[rows] = jnp.exp2(m_old - m_new)
                m_ref[rows] = m_new
                for c in range(kb // LANES):
                    cols = pl.ds(c * LANES, LANES)
                    p_ref[rows, cols] = jnp.exp2(s_ref[sub, cols] - m_new).astype(BF16)
            h_rows = pl.ds(h * Q_BLOCK, Q_BLOCK)
            pv = jnp.dot(p_ref[h_rows], kv_ref[0, keys_of_block, :],
                         preferred_element_type=F32)
            a = alpha_ref[h_rows]
            acc_ref[h_rows, lo] = a * acc_ref[h_rows, lo] + pv[:, :KV_RANK]
            acc_ref[h_rows, hi] = a * acc_ref[h_rows, hi] + pv[:, KV_RANK:]
        return carry

    lax.fori_loop(0, nkb, attn_block, 0)

    o = (acc_ref[:, pl.ds(0, KV_RANK)] / acc_ref[:, pl.ds(KV_RANK, 1)]).astype(BF16)
    y = jnp.zeros((Q_BLOCK, ATTN_WIDTH), F32)
    for h in range(N_HEADS):
        y = y + jnp.dot(o[h * Q_BLOCK:(h + 1) * Q_BLOCK], wuvp_ref[h],
                        preferred_element_type=F32)
    yb_ref[...] = y.astype(BF16)


def _dsa(q_hm, qi_hm, wi, kv, ki, wuvp, batch, seq):
    nq = seq // Q_BLOCK
    kb = min(KEY_BLOCK, seq)
    k_top = min(TOPK_MAX, seq // 4)
    rows = N_HEADS * Q_BLOCK
    kern = functools.partial(_dsa_kernel, kb=kb, k_top=float(k_top))
    return pl.pallas_call(
        kern,
        grid=(batch, nq),
        in_specs=[
            pl.BlockSpec((1, N_HEADS, Q_BLOCK, KV_RANK), lambda b, i: (b * nq + i, 0, 0, 0)),
            pl.BlockSpec((1, IDX_HEADS, Q_BLOCK, LANES), lambda b, i: (b * nq + i, 0, 0, 0)),
            pl.BlockSpec((Q_BLOCK, LANES), lambda b, i: (b * nq + i, 0)),
            pl.BlockSpec((1, seq, 2 * KV_RANK), lambda b, i: (b, 0, 0)),
            pl.BlockSpec((1, seq, LANES), lambda b, i: (b, 0, 0)),
            _full_spec(wuvp),
        ],
        out_specs=pl.BlockSpec((Q_BLOCK, ATTN_WIDTH), lambda b, i: (b * nq + i, 0)),
        out_shape=jax.ShapeDtypeStruct((batch * seq, ATTN_WIDTH), BF16),
        scratch_shapes=[
            pltpu.VMEM((seq // kb, Q_BLOCK, kb), I32),
            pltpu.VMEM((rows, LANES), F32),
            pltpu.VMEM((rows, LANES), F32),
            pltpu.VMEM((rows, 2 * KV_RANK), F32),
            pltpu.VMEM((rows, kb), BF16),
            pltpu.VMEM((Q_BLOCK, kb), F32),
            pltpu.VMEM((Q_BLOCK, kb), F32),
            pltpu.VMEM((Q_BLOCK, kb), F32),
            pltpu.VMEM((Q_BLOCK, LANES), I32),
            pltpu.VMEM((Q_BLOCK, LANES), F32),
            pltpu.VMEM((Q_BLOCK, LANES), F32),
            pltpu.VMEM((seq // kb, Q_BLOCK, kb), I16),
            pltpu.VMEM((seq // kb, Q_BLOCK, kb), I16),
            pltpu.VMEM((seq // kb, Q_BLOCK, kb), I16),
        ],
        compiler_params=pltpu.CompilerParams(
            dimension_semantics=("arbitrary", "arbitrary"), vmem_limit_bytes=VMEM_LIMIT_BYTES),
        name="dsa",
    )(q_hm, qi_hm, wi, kv.reshape(batch, seq, 2 * KV_RANK), ki.reshape(batch, seq, LANES), wuvp)


def _merge_kernel(x_ref, pa_ref, gb_ref, yb_ref, wpb_ref, wout_ref, g2_ref, wr_ref, br_ref,
                  x2_ref, xn2_ref, te_ref, gate_ref, rank_ref, cnt_ref, carry_ref):
    tm = x_ref.shape[0]

    @pl.when(pl.program_id(0) == 0)
    def _():
        carry_ref[...] = jnp.zeros(carry_ref.shape, F32)

    pb = jnp.dot(yb_ref[...], wpb_ref[...], preferred_element_type=F32)
    merged = pa_ref[...].astype(F32) + gb_ref[...].astype(F32) * pb
    x2 = x_ref[...] + jnp.dot(merged.astype(BF16), wout_ref[...], preferred_element_type=F32)
    x2_ref[...] = x2
    xn2 = _rms(x2) * g2_ref[...]
    xn2_ref[...] = xn2

    logits = jnp.dot(xn2, wr_ref[...], preferred_element_type=F32,
                     precision=lax.Precision.HIGHEST) + br_ref[...]
    lane = lax.broadcasted_iota(I32, (tm, LANES), 1).astype(F32)
    work = jnp.where(lane < N_EXPERTS, logits, -jnp.inf)
    onehot = jnp.zeros((tm, LANES), F32)
    vals, idxs = [], []
    for _ in range(TOP_K):
        mx = jnp.max(work, axis=1, keepdims=True)
        ix = jnp.min(jnp.where(work == mx, lane, float(LANES)), axis=1, keepdims=True)
        hit = lane == ix
        onehot = onehot + jnp.where(hit, 1.0, 0.0)
        work = jnp.where(hit, -jnp.inf, work)
        vals.append(mx)
        idxs.append(ix)
    exps = [jnp.exp(v - vals[0]) for v in vals]
    denom = exps[0] + exps[1] + exps[2] + exps[3]

    r_io = lax.broadcasted_iota(I32, (tm, tm), 0)
    c_io = lax.broadcasted_iota(I32, (tm, tm), 1)
    lower = jnp.where(c_io <= r_io, 1.0, 0.0).astype(BF16)
    pref = jnp.dot(lower, onehot.astype(BF16), preferred_element_type=F32)
    base = pref - onehot + carry_ref[...]
    te = jnp.zeros((tm, LANES), F32)
    gate = jnp.zeros((tm, LANES), F32)
    rank = jnp.zeros((tm, LANES), F32)
    for k in range(TOP_K):
        slot = lane == float(k)
        rk = jnp.sum(jnp.where(lane == idxs[k], base, 0.0), axis=1, keepdims=True)
        te = jnp.where(slot, idxs[k], te)
        gate = jnp.where(slot, exps[k] / denom, gate)
        rank = jnp.where(slot, rk, rank)
    te_ref[...] = te.astype(I32)
    gate_ref[...] = gate
    rank_ref[...] = rank.astype(I32)
    carry_ref[...] = carry_ref[...] + pref[tm - 1:tm, :]
    cnt_ref[...] = carry_ref[...]


def _merge(x2d, pa, gb, yb, wpb, wout, g2, wr, br):
    t, d = x2d.shape
    tm = min(TM_MERGE, t)
    row = lambda w: pl.BlockSpec((tm, w), lambda i: (i, 0))
    weights = (wpb, wout, g2, wr, br)
    out_shape = (
        jax.ShapeDtypeStruct((t, d), F32),
        jax.ShapeDtypeStruct((t, d), F32),
        jax.ShapeDtypeStruct((t, LANES), I32),
        jax.ShapeDtypeStruct((t, LANES), F32),
        jax.ShapeDtypeStruct((t, LANES), I32),
        jax.ShapeDtypeStruct((1, LANES), F32),
    )
    return pl.pallas_call(
        _merge_kernel,
        grid=(t // tm,),
        in_specs=[row(d), row(d), row(d), row(ATTN_WIDTH)] + [_full_spec(w) for w in weights],
        out_specs=(row(d), row(d), row(LANES), row(LANES), row(LANES),
                   pl.BlockSpec((1, LANES), lambda i: (0, 0))),
        out_shape=out_shape,
        scratch_shapes=[pltpu.VMEM((1, LANES), F32)],
        compiler_params=pltpu.CompilerParams(
            dimension_semantics=("arbitrary",), vmem_limit_bytes=VMEM_LIMIT_BYTES),
        name="merge",
    )(x2d, pa, gb, yb, *weights)


def _row_copy(src, src_row, dst, dst_row, sem):
    return pltpu.make_async_copy(src.at[pl.ds(src_row, 1)], dst.at[pl.ds(dst_row, 1)], sem)


def _dispatch_kernel(pstart_ref, cnt_ref, nused_ref, te_ref, rank_ref, xn2_ref, xs_ref,
                     zero_ref, sem, zsem):
    tm = xn2_ref.shape[0]
    n_blocks = xs_ref.shape[0] // MOE_BLOCK

    @pl.when(pl.program_id(0) == 0)
    def _():
        zero_ref[...] = jnp.zeros(zero_ref.shape, F32)

        def pad_expert(e, carry):
            cnt = cnt_ref[e]
            n_pad = (cnt + MOE_BLOCK - 1) // MOE_BLOCK * MOE_BLOCK - cnt
            first = pstart_ref[e] + cnt

            def start(r, c):
                _row_copy(zero_ref, 0, xs_ref, first + r, zsem).start()
                return c

            def wait(r, c):
                _row_copy(zero_ref, 0, xs_ref, 0, zsem).wait()
                return c

            lax.fori_loop(0, n_pad, start, 0)
            lax.fori_loop(0, n_pad, wait, 0)
            return carry

        lax.fori_loop(0, N_EXPERTS, pad_expert, 0)

        def zero_block(b, carry):
            cp = pltpu.make_async_copy(
                zero_ref, xs_ref.at[pl.ds(pl.multiple_of(b * MOE_BLOCK, MOE_BLOCK), MOE_BLOCK)], zsem)
            cp.start()
            cp.wait()
            return carry

        lax.fori_loop(nused_ref[0], n_blocks, zero_block, 0)

    def issue(t, carry):
        for k in range(TOP_K):
            dest = pstart_ref[te_ref[t * TOP_K + k]] + rank_ref[t * TOP_K + k]
            _row_copy(xn2_ref, t, xs_ref, dest, sem).start()
        return carry

    lax.fori_loop(0, tm, issue, 0)

    def drain(t, carry):
        for k in range(TOP_K):
            _row_copy(xn2_ref, 0, xs_ref, 0, sem).wait()
        return carry

    lax.fori_loop(0, tm, drain, 0)


def _dispatch(pstart, counts, n_used, te_flat, rank_flat, xn2, n_rows):
    t, d = xn2.shape
    tm = min(TM_ROWS, t)
    smem_blk = pl.BlockSpec((tm * TOP_K,), lambda i, *_: (i,), memory_space=pltpu.SMEM)
    grid_spec = pltpu.PrefetchScalarGridSpec(
        num_scalar_prefetch=3,
        grid=(t // tm,),
        in_specs=[smem_blk, smem_blk, pl.BlockSpec((tm, d), lambda i, *_: (i, 0))],
        out_specs=pl.BlockSpec(memory_space=pl.ANY),
        scratch_shapes=[pltpu.VMEM((MOE_BLOCK, d), F32),
                        pltpu.SemaphoreType.DMA(()), pltpu.SemaphoreType.DMA(())],
    )
    return pl.pallas_call(
        _dispatch_kernel,
        grid_spec=grid_spec,
        out_shape=jax.ShapeDtypeStruct((n_rows, d), F32),
        compiler_params=pltpu.CompilerParams(dimension_semantics=("arbitrary",)),
        name="dispatch",
    )(pstart, counts, n_used, te_flat, rank_flat, xn2)


def _moe_kernel(blk_e_ref, xs_ref, w1_ref, b1_ref, w2_ref, b2_ref, y_ref, w1b_ref, w2b_ref):
    i = pl.program_id(0)
    prev_e = blk_e_ref[jnp.maximum(i - 1, 0)]

    @pl.when(jnp.logical_or(i == 0, blk_e_ref[i] != prev_e))
    def _():
        def cast(src_ref, dst_ref):
            def body(r, c):
                rows = pl.ds(pl.multiple_of(r * LANES, LANES), LANES)
                dst_ref[rows, :] = src_ref[0, rows, :].astype(BF16)
                return c
            lax.fori_loop(0, src_ref.shape[1] // LANES, body, 0)

        cast(w1_ref, w1b_ref)
        cast(w2_ref, w2b_ref)

    xb = xs_ref[...].astype(BF16)
    hb = jnp.dot(xb, w1b_ref[...], preferred_element_type=F32) + b1_ref[0]
    hg = jnp.minimum(hb[:, :D_EXPERT], SWIGLU_LIMIT)
    hu = jnp.clip(hb[:, D_EXPERT:], -SWIGLU_LIMIT, SWIGLU_LIMIT)
    act = (hu + 1.0) * (hg * jax.nn.sigmoid(SWIGLU_ALPHA * hg))
    y_ref[...] = jnp.dot(act.astype(BF16), w2b_ref[...], preferred_element_type=F32) + b2_ref[0]


def _moe(blk_e, xs, w1, b1, w2, b2):
    n_rows, d = xs.shape
    n_blocks = n_rows // MOE_BLOCK
    grid_spec = pltpu.PrefetchScalarGridSpec(
        num_scalar_prefetch=1,
        grid=(n_blocks,),
        in_specs=[
            pl.BlockSpec((MOE_BLOCK, d), lambda i, be: (i, 0)),
            pl.BlockSpec((1, d, 2 * D_EXPERT), lambda i, be: (be[i], 0, 0)),
            pl.BlockSpec((1, 1, 2 * D_EXPERT), lambda i, be: (be[i], 0, 0)),
            pl.BlockSpec((1, D_EXPERT, d), lambda i, be: (be[i], 0, 0)),
            pl.BlockSpec((1, 1, d), lambda i, be: (be[i], 0, 0)),
        ],
        out_specs=pl.BlockSpec((MOE_BLOCK, d), lambda i, be: (i, 0)),
        scratch_shapes=[pltpu.VMEM((d, 2 * D_EXPERT), BF16), pltpu.VMEM((D_EXPERT, d), BF16)],
    )
    return pl.pallas_call(
        _moe_kernel,
        grid_spec=grid_spec,
        out_shape=jax.ShapeDtypeStruct((n_rows, d), F32),
        compiler_params=pltpu.CompilerParams(
            dimension_semantics=("arbitrary",), vmem_limit_bytes=VMEM_LIMIT_BYTES),
        name="moe",
    )(blk_e, xs, w1, b1, w2, b2)


def _combine_kernel(pstart_ref, te_ref, rank_ref, x2_ref, gate_ref, ys_ref, out_ref, buf_ref, sem):
    tm = x2_ref.shape[0]

    def issue(t, carry):
        for k in range(TOP_K):
            src = pstart_ref[te_ref[t * TOP_K + k]] + rank_ref[t * TOP_K + k]
            _row_copy(ys_ref, src, buf_ref.at[k], t, sem).start()
        return carry

    lax.fori_loop(0, tm, issue, 0)

    def drain(t, carry):
        for k in range(TOP_K):
            _row_copy(ys_ref, 0, buf_ref.at[k], 0, sem).wait()
        return carry

    lax.fori_loop(0, tm, drain, 0)

    gate = gate_ref[...]
    out = x2_ref[...]
    for k in range(TOP_K):
        out = out + gate[:, k:k + 1] * buf_ref[k]
    out_ref[...] = out


def _combine(pstart, te_flat, rank_flat, x2, gate, ys):
    t, d = x2.shape
    tm = min(TM_ROWS, t)
    smem_blk = pl.BlockSpec((tm * TOP_K,), lambda i, ps: (i,), memory_space=pltpu.SMEM)
    grid_spec = pltpu.PrefetchScalarGridSpec(
        num_scalar_prefetch=1,
        grid=(t // tm,),
        in_specs=[smem_blk, smem_blk,
                  pl.BlockSpec((tm, d), lambda i, ps: (i, 0)),
                  pl.BlockSpec((tm, LANES), lambda i, ps: (i, 0)),
                  pl.BlockSpec(memory_space=pl.ANY)],
        out_specs=pl.BlockSpec((tm, d), lambda i, ps: (i, 0)),
        scratch_shapes=[pltpu.VMEM((TOP_K, tm, d), F32), pltpu.SemaphoreType.DMA(())],
    )
    return pl.pallas_call(
        _combine_kernel,
        grid_spec=grid_spec,
        out_shape=jax.ShapeDtypeStruct((t, d), F32),
        compiler_params=pltpu.CompilerParams(
            dimension_semantics=("arbitrary",), vmem_limit_bytes=VMEM_LIMIT_BYTES),
        name="combine",
    )(pstart, te_flat, rank_flat, x2, gate, ys)


def kernel(x, norm1_g, w_in, q_lat_g, w_uq, q_norm_g, kv_norm_g, w_uv, w_q_idx, k_idx_g,
           gmlp_v_g, w_spatial, b_spatial, w_proj_a, w_proj_b, w_out, norm2_g,
           w_router, b_router, w_exp1, b_exp1, w_exp2, b_exp2):
    batch, seq, d = x.shape
    t = batch * seq
    assert d == D_MODEL and seq % min(KEY_BLOCK, seq) == 0 and seq % Q_BLOCK == 0
    assert t % TM_INPROJ == 0 and t % TM_MERGE == 0 and (t * TOP_K) % MOE_BLOCK == 0
    x2d = x.reshape(t, d)
    row = lambda v: v.reshape(1, -1).astype(F32)

    wuv = w_in[:, OFF_UV:OFF_CQ].astype(BF16)
    wcq = w_in[:, OFF_CQ:OFF_CKV].astype(BF16)
    wmisc = jnp.pad(w_in[:, OFF_CKV:OFF_GATE], ((0, 0), (0, 2 * LANES - (OFF_GATE - OFF_CKV)))
                    ).astype(BF16)
    wga = w_in[:, OFF_GATE:OFF_GATE + D_MODEL].astype(BF16)
    wgb = w_in[:, OFF_GATE + D_MODEL:IN_WIDTH].astype(BF16)
    wuq = w_uq.reshape(Q_RANK, N_HEADS * KV_RANK).astype(BF16)
    wqi = jnp.pad(w_q_idx * (IDX_DIM ** -0.5), ((0, 0), (0, 0), (0, LANES - IDX_DIM))
                  ).reshape(Q_RANK, IDX_HEADS * LANES).astype(BF16)
    kidx_g = jnp.pad(k_idx_g, (0, LANES - IDX_DIM)).reshape(1, LANES).astype(F32)
    qn_g = row(q_norm_g) * (KV_RANK ** -0.5 * LOG2E)
    wuvp = jnp.zeros((N_HEADS, KV_RANK, ATTN_WIDTH), F32)
    for h in range(N_HEADS):
        wuvp = wuvp.at[h, :, h * HEAD_V:(h + 1) * HEAD_V].set(w_uv[h])
    wuvp = wuvp.astype(BF16)

    pa, gb, q_hm, qi_hm, kv, ki, wi = _inproj(
        x2d, row(norm1_g), wuv, wcq, wmisc, wga, wgb, row(q_lat_g), wuq, qn_g,
        row(kv_norm_g), wqi, kidx_g, row(gmlp_v_g), w_spatial.astype(F32),
        b_spatial.T.astype(F32), w_proj_a.astype(BF16))

    yb = _dsa(q_hm, qi_hm, wi, kv, ki, wuvp, batch, seq)

    wr = jnp.pad(w_router, ((0, 0), (0, LANES - N_EXPERTS))).astype(F32)
    br = jnp.pad(b_router, (0, LANES - N_EXPERTS)).reshape(1, LANES).astype(F32)
    x2, xn2, te, gate, rank, cnt = _merge(
        x2d, pa, gb, yb, w_proj_b.astype(BF16), w_out.astype(BF16), row(norm2_g), wr, br)

    counts = cnt[0, :N_EXPERTS].astype(I32)
    padded = (counts + MOE_BLOCK - 1) // MOE_BLOCK * MOE_BLOCK
    pad_end = jnp.cumsum(padded)
    pstart = (pad_end - padded).astype(I32)
    n_blocks = (t * TOP_K) // MOE_BLOCK + N_EXPERTS
    blk_start = jnp.arange(n_blocks, dtype=I32) * MOE_BLOCK
    blk_e = jnp.minimum(jnp.sum((pad_end[None, :] <= blk_start[:, None]).astype(I32), axis=1),
                        N_EXPERTS - 1).astype(I32)
    n_used = (pad_end[N_EXPERTS - 1:] // MOE_BLOCK).astype(I32)
    te_flat = te[:, :TOP_K].reshape(t * TOP_K)
    rank_flat = rank[:, :TOP_K].reshape(t * TOP_K)

    xs = _dispatch(pstart, counts, n_used, te_flat, rank_flat, xn2, n_blocks * MOE_BLOCK)
    ys = _moe(blk_e, xs, w_exp1.astype(F32), b_exp1.reshape(N_EXPERTS, 1, -1).astype(F32),
              w_exp2.astype(F32), b_exp2.reshape(N_EXPERTS, 1, -1).astype(F32))
    out = _combine(pstart, te_flat, rank_flat, x2, gate, ys)
    return out.reshape(batch, seq, d)
```

```python
import functools
import math

import jax
import jax.numpy as jnp
from jax import lax
from jax.experimental import pallas as pl
from jax.experimental.pallas import tpu as pltpu

F32 = jnp.float32
BF16 = jnp.bfloat16
I32 = jnp.int32

D_MODEL = 1024
GMLP_GROUPS = 4
GMLP_GROUP_DIM = 128
GMLP_WIDTH = GMLP_GROUPS * GMLP_GROUP_DIM
CHUNK = 128
N_HEADS = 8
Q_RANK = 256
KV_RANK = 128
HEAD_V = 64
ATTN_WIDTH = N_HEADS * HEAD_V
IDX_HEADS = 4
IDX_DIM = 64
TOPK_MAX = 256
N_EXPERTS = 32
TOP_K = 4
D_EXPERT = 1024
SWIGLU_LIMIT = 7.0
SWIGLU_ALPHA = 1.702
EPS = 1e-6

OFF_UV = 0
OFF_CQ = OFF_UV + 2 * GMLP_WIDTH
OFF_CKV = OFF_CQ + Q_RANK
OFF_KIDX = OFF_CKV + KV_RANK
OFF_WIDX = OFF_KIDX + IDX_DIM
OFF_GATE = OFF_WIDX + IDX_HEADS
IN_WIDTH = OFF_GATE + 2 * D_MODEL

LANES = 128
SUBLANES = 8
TILE_ROWS = D_MODEL // LANES
Q_BLOCK = 256
VMEM_LIMIT_BYTES = 56 * 1024 * 1024

TM_INPROJ = 512
KEY_BLOCK = 1024
COUNT_ROWS = 64
COUNT_BLOCK = 512
SOFTMAX_ROWS = 64
TM_MERGE = 512
TM_ROWS = 256
MOE_BLOCK = 256
ISSUE_UNROLL = 4

INT_MIN = -(2 ** 31)
NEG_BIG = -1e30
LOG2E = 1.4426950408889634


def _full_spec(arr):
    nd = arr.ndim
    return pl.BlockSpec(arr.shape, lambda *_: (0,) * nd)


def _rms(v, axis=-1):
    return v * lax.rsqrt(jnp.mean(v * v, axis=axis, keepdims=True) + EPS)


def _gelu_tanh(v):
    c = math.sqrt(2.0 / math.pi)
    return 0.5 * v * (1.0 + jnp.tanh(c * (v + 0.044715 * (v * v * v))))


def _inproj_kernel(x_ref, g1_ref, wuv_ref, wcq_ref, wmisc_ref, wga_ref, wgb_ref,
                   qlat_g_ref, wuq_ref, qn_g_ref, kvn_g_ref, wqi_ref, kidx_g_ref,
                   vg_ref, wsp_ref, bsp_ref, wpa_ref,
                   pa_ref, gb_ref, q_ref, qi_ref, kv_ref, ki_ref, wi_ref,
                   ya_ref):
    tm = x_ref.shape[0]
    x = x_ref[...]
    xn = (_rms(x) * g1_ref[...]).astype(BF16)

    z = _gelu_tanh(jnp.dot(xn, wuv_ref[...], preferred_element_type=F32))
    r_io = lax.broadcasted_iota(I32, (CHUNK, CHUNK), 0)
    c_io = lax.broadcasted_iota(I32, (CHUNK, CHUNK), 1)
    causal = c_io <= r_io
    for g in range(GMLP_GROUPS):
        cols = slice(g * GMLP_GROUP_DIM, (g + 1) * GMLP_GROUP_DIM)
        u_g = z[:, cols]
        v_g = z[:, GMLP_WIDTH + g * GMLP_GROUP_DIM:GMLP_WIDTH + (g + 1) * GMLP_GROUP_DIM]
        vn_g = (_rms(v_g) * vg_ref[...]).astype(BF16)
        w_g = jnp.where(causal, wsp_ref[g], 0.0).astype(BF16)
        b_g = bsp_ref[:, g:g + 1]
        for c in range(tm // CHUNK):
            rows = slice(c * CHUNK, (c + 1) * CHUNK)
            sv = jnp.dot(w_g, vn_g[rows], preferred_element_type=F32) + b_g
            ya_ref[rows, cols] = (u_g[rows] * sv).astype(BF16)
    gate_a = jax.nn.sigmoid(jnp.dot(xn, wga_ref[...], preferred_element_type=F32))
    pa = gate_a * jnp.dot(ya_ref[...], wpa_ref[...], preferred_element_type=F32)
    pa_ref[...] = pa.astype(BF16)
    gb_ref[...] = jax.nn.sigmoid(
        jnp.dot(xn, wgb_ref[...], preferred_element_type=F32)).astype(BF16)

    c_q = (_rms(jnp.dot(xn, wcq_ref[...], preferred_element_type=F32))
           * qlat_g_ref[...]).astype(BF16)
    qf = jnp.dot(c_q, wuq_ref[...], preferred_element_type=F32)
    qif = jnp.dot(c_q, wqi_ref[...], preferred_element_type=F32)
    for h in range(N_HEADS):
        qh = (_rms(qf[:, h * KV_RANK:(h + 1) * KV_RANK]) * qn_g_ref[...]).astype(BF16)
        for j in range(tm // Q_BLOCK):
            q_ref[j, h] = qh[j * Q_BLOCK:(j + 1) * Q_BLOCK]
    for h in range(IDX_HEADS):
        qih = qif[:, h * LANES:(h + 1) * LANES].astype(BF16)
        for j in range(tm // Q_BLOCK):
            qi_ref[j, h] = qih[j * Q_BLOCK:(j + 1) * Q_BLOCK]

    hm = jnp.dot(xn, wmisc_ref[...], preferred_element_type=F32)
    hk = hm[:, KV_RANK:]
    lane = lax.broadcasted_iota(I32, hk.shape, 1)
    kv_ref[:, :KV_RANK] = (_rms(hm[:, :KV_RANK]) * kvn_g_ref[...]).astype(BF16)
    kv_ref[:, KV_RANK:] = jnp.where(lane == 0, 1.0, 0.0).astype(BF16)
    is_k = lane < IDX_DIM
    ms_k = jnp.sum(jnp.where(is_k, hk * hk, 0.0), axis=-1, keepdims=True) * (1.0 / IDX_DIM)
    ki = jnp.where(is_k, hk * lax.rsqrt(ms_k + EPS) * kidx_g_ref[...], 0.0)
    ki_ref[...] = ki.astype(BF16)
    wi_ref[...] = hk * (IDX_HEADS ** -0.5)


def _inproj(x2d, g1, wuv, wcq, wmisc, wga, wgb, qlat_g, wuq, qn_g, kvn_g, wqi, kidx_g,
            vg, wsp, bsp_t, wpa):
    t, d = x2d.shape
    tm = min(TM_INPROJ, t)
    nqb = t // Q_BLOCK
    weights = (g1, wuv, wcq, wmisc, wga, wgb, qlat_g, wuq, qn_g, kvn_g, wqi, kidx_g,
               vg, wsp, bsp_t, wpa)
    out_shape = (
        jax.ShapeDtypeStruct((t, d), BF16),
        jax.ShapeDtypeStruct((t, d), BF16),
        jax.ShapeDtypeStruct((nqb, N_HEADS, Q_BLOCK, KV_RANK), BF16),
        jax.ShapeDtypeStruct((nqb, IDX_HEADS, Q_BLOCK, LANES), BF16),
        jax.ShapeDtypeStruct((t, 2 * KV_RANK), BF16),
        jax.ShapeDtypeStruct((t, LANES), BF16),
        jax.ShapeDtypeStruct((t, LANES), F32),
    )
    row = lambda w: pl.BlockSpec((tm, w), lambda i: (i, 0))
    out_specs = (
        row(d), row(d),
        pl.BlockSpec((tm // Q_BLOCK, N_HEADS, Q_BLOCK, KV_RANK), lambda i: (i, 0, 0, 0)),
        pl.BlockSpec((tm // Q_BLOCK, IDX_HEADS, Q_BLOCK, LANES), lambda i: (i, 0, 0, 0)),
        row(2 * KV_RANK), row(LANES), row(LANES),
    )
    return pl.pallas_call(
        _inproj_kernel,
        grid=(t // tm,),
        in_specs=[row(d)] + [_full_spec(w) for w in weights],
        out_specs=out_specs,
        out_shape=out_shape,
        scratch_shapes=[pltpu.VMEM((tm, GMLP_WIDTH), BF16)],
        compiler_params=pltpu.CompilerParams(
            dimension_semantics=("arbitrary",), vmem_limit_bytes=VMEM_LIMIT_BYTES),
        name="inproj",
    )(x2d, *weights)


def _dsa_kernel(q_ref, qi_ref, wi_ref, kv_ref, ki_ref, wuvp_ref, yb_ref,
                keys_ref, m_ref, alpha_ref, acc_ref, p_ref, bias_ref, s0_ref, s1_ref,
                thr_ref, need_ref, run_ref, keyst_ref,
                *, kb, k_top):
    qb = pl.program_id(1)
    q0 = qb * Q_BLOCK
    nkb = (q0 + Q_BLOCK + kb - 1) // kb
    nt = (((1,), (1,)), ((), ()))

    qidx = qi_ref[0].reshape(IDX_HEADS * Q_BLOCK, LANES)
    wslab = wi_ref[...]
    w_cols = [wslab[:, IDX_DIM + h:IDX_DIM + h + 1] for h in range(IDX_HEADS)]
    row_t = q0 + lax.broadcasted_iota(I32, (Q_BLOCK, 1), 0)

    def score_block(j, carry):
        kblk = ki_ref[0, pl.ds(pl.multiple_of(j * kb, kb), kb), :]
        lg = lax.dot_general(qidx, kblk, nt, preferred_element_type=F32)
        sc = jnp.zeros((Q_BLOCK, kb), F32)
        for h in range(IDX_HEADS):
            sc = sc + jnp.maximum(lg[h * Q_BLOCK:(h + 1) * Q_BLOCK], 0.0) * w_cols[h]
        sc = jnp.where(sc == 0.0, 0.0, sc)
        bits = pltpu.bitcast(sc, I32)
        key = bits ^ ((bits >> 31) & 0x7FFFFFFF)
        kpos = j * kb + lax.broadcasted_iota(I32, (1, kb), 1)
        key = jnp.where(kpos <= row_t, key, INT_MIN)
        keys_ref[j] = key
        key_t = key.T
        for s in range(kb // COUNT_BLOCK):
            keyst_ref[j * (kb // COUNT_BLOCK) + s] = key_t[s * COUNT_BLOCK:(s + 1) * COUNT_BLOCK]
        return carry

    lax.fori_loop(0, nkb, score_block, 0)

    def count(cmp, thr):
        thr_b = jnp.broadcast_to(thr, (COUNT_ROWS, Q_BLOCK))

        def body(j, acc):
            for c in range(COUNT_BLOCK // COUNT_ROWS):
                k = keyst_ref[j, pl.ds(c * COUNT_ROWS, COUNT_ROWS), :]
                acc = acc + jnp.where(cmp(k, thr_b), 1.0, 0.0)
            return acc

        n_count = (q0 + Q_BLOCK + COUNT_BLOCK - 1) // COUNT_BLOCK
        acc = lax.fori_loop(0, n_count, body, jnp.zeros((COUNT_ROWS, Q_BLOCK), F32))
        return jnp.sum(acc, axis=0, keepdims=True)

    ge = lambda a, b: a >= b
    gt = lambda a, b: a > b

    def bit_step(i, thr):
        cand = thr + lax.shift_left(jnp.int32(1), 31 - i)
        return jnp.where(count(ge, cand) >= k_top, cand, thr)

    thr = lax.fori_loop(0, 32, bit_step, jnp.full((1, Q_BLOCK), INT_MIN, I32))

    thr = jnp.maximum(thr, INT_MIN + 1)
    has_ties = jnp.max(count(ge, thr)) > k_top
    to_rows = lambda v: jnp.broadcast_to(v, (LANES, Q_BLOCK)).T
    thr_ref[...] = to_rows(thr)

    @pl.when(has_ties)
    def _():
        need_ref[...] = to_rows(k_top - count(gt, thr))
        run_ref[...] = jnp.zeros(run_ref.shape, F32)

    def bias_ties(j):
        k = keys_ref[j]
        t = thr_ref[:, 0:1]
        eq = k == t
        r_io = lax.broadcasted_iota(I32, (kb, kb), 0)
        c_io = lax.broadcasted_iota(I32, (kb, kb), 1)
        upper = jnp.where(r_io <= c_io, 1.0, 0.0).astype(BF16)
        pref = jnp.dot(jnp.where(eq, 1.0, 0.0).astype(BF16), upper,
                       preferred_element_type=F32)
        run = run_ref[:, 0:1]
        sel = (k > t) | (eq & (pref + run <= need_ref[:, 0:1]))
        bias_ref[...] = jnp.where(sel, 0.0, NEG_BIG)
        run_ref[...] = jnp.broadcast_to(run + pref[:, kb - 1:kb], run_ref.shape)

    def bias_plain(j):
        for r in range(Q_BLOCK // SOFTMAX_ROWS):
            sub = pl.ds(r * SOFTMAX_ROWS, SOFTMAX_ROWS)
            t = thr_ref[sub]
            for c in range(kb // LANES):
                cols = pl.ds(c * LANES, LANES)
                bias_ref[sub, cols] = jnp.where(keys_ref[j, sub, cols] >= t, 0.0, NEG_BIG)

    m_ref[...] = jnp.full(m_ref.shape, NEG_BIG, F32)
    acc_ref[...] = jnp.zeros(acc_ref.shape, F32)
    s_refs = (s0_ref, s1_ref)
    lo, hi = pl.ds(0, KV_RANK), pl.ds(KV_RANK, KV_RANK)

    def attn_block(j, carry):
        keys_of_block = pl.ds(pl.multiple_of(j * kb, kb), kb)
        lax.cond(has_ties, bias_ties, bias_plain, j)
        for h in range(N_HEADS):
            s_ref = s_refs[h % 2]
            s_ref[...] = lax.dot_general(q_ref[0, h], kv_ref[0, keys_of_block, lo], nt,
                                         preferred_element_type=F32)
            for r in range(Q_BLOCK // SOFTMAX_ROWS):
                sub = pl.ds(r * SOFTMAX_ROWS, SOFTMAX_ROWS)
                rows = pl.ds(h * Q_BLOCK + r * SOFTMAX_ROWS, SOFTMAX_ROWS)
                s = s_ref[sub, :] + bias_ref[sub, :]
                s_ref[sub, :] = s
                m_old = m_ref[rows]
                m_new = jnp.maximum(m_old, jnp.max(s, axis=1, keepdims=True))
                alpha_ref[rows] = jnp.exp2(m_old - m_new)
                m_ref[rows] = m_new
                for c in range(kb // LANES):
                    cols = pl.ds(c * LANES, LANES)
                    p_ref[rows, cols] = jnp.exp2(s_ref[sub, cols] - m_new).astype(BF16)
            h_rows = pl.ds(h * Q_BLOCK, Q_BLOCK)
            pv = jnp.dot(p_ref[h_rows], kv_ref[0, keys_of_block, :],
                         preferred_element_type=F32)
            a = alpha_ref[h_rows]
            acc_ref[h_rows, lo] = a * acc_ref[h_rows, lo] + pv[:, :KV_RANK]
            acc_ref[h_rows, hi] = a * acc_ref[h_rows, hi] + pv[:, KV_RANK:]
        return carry

    lax.fori_loop(0, nkb, attn_block, 0)

    o = (acc_ref[:, pl.ds(0, KV_RANK)] / acc_ref[:, pl.ds(KV_RANK, 1)]).astype(BF16)
    y = jnp.zeros((Q_BLOCK, ATTN_WIDTH), F32)
    for h in range(N_HEADS):
        y = y + jnp.dot(o[h * Q_BLOCK:(h + 1) * Q_BLOCK], wuvp_ref[h],
                        preferred_element_type=F32)
    yb_ref[...] = y.astype(BF16)


def _dsa(q_hm, qi_hm, wi, kv, ki, wuvp, batch, seq):
    nq = seq // Q_BLOCK
    kb = min(KEY_BLOCK, seq)
    k_top = min(TOPK_MAX, seq // 4)
    rows = N_HEADS * Q_BLOCK
    kern = functools.partial(_dsa_kernel, kb=kb, k_top=float(k_top))
    return pl.pallas_call(
        kern,
        grid=(batch, nq),
        in_specs=[
            pl.BlockSpec((1, N_HEADS, Q_BLOCK, KV_RANK), lambda b, i: (b * nq + i, 0, 0, 0)),
            pl.BlockSpec((1, IDX_HEADS, Q_BLOCK, LANES), lambda b, i: (b * nq + i, 0, 0, 0)),
            pl.BlockSpec((Q_BLOCK, LANES), lambda b, i: (b * nq + i, 0)),
            pl.BlockSpec((1, seq, 2 * KV_RANK), lambda b, i: (b, 0, 0)),
            pl.BlockSpec((1, seq, LANES), lambda b, i: (b, 0, 0)),
            _full_spec(wuvp),
        ],
        out_specs=pl.BlockSpec((Q_BLOCK, ATTN_WIDTH), lambda b, i: (b * nq + i, 0)),
        out_shape=jax.ShapeDtypeStruct((batch * seq, ATTN_WIDTH), BF16),
        scratch_shapes=[
            pltpu.VMEM((seq // kb, Q_BLOCK, kb), I32),
            pltpu.VMEM((rows, LANES), F32),
            pltpu.VMEM((rows, LANES), F32),
            pltpu.VMEM((rows, 2 * KV_RANK), F32),
            pltpu.VMEM((rows, kb), BF16),
            pltpu.VMEM((Q_BLOCK, kb), F32),
            pltpu.VMEM((Q_BLOCK, kb), F32),
            pltpu.VMEM((Q_BLOCK, kb), F32),
            pltpu.VMEM((Q_BLOCK, LANES), I32),
            pltpu.VMEM((Q_BLOCK, LANES), F32),
            pltpu.VMEM((Q_BLOCK, LANES), F32),
            pltpu.VMEM((seq // COUNT_BLOCK, COUNT_BLOCK, Q_BLOCK), I32),
        ],
        compiler_params=pltpu.CompilerParams(
            dimension_semantics=("arbitrary", "arbitrary"), vmem_limit_bytes=VMEM_LIMIT_BYTES),
        name="dsa",
    )(q_hm, qi_hm, wi, kv.reshape(batch, seq, 2 * KV_RANK), ki.reshape(batch, seq, LANES), wuvp)


def _store_token_tiles(ref, value):
    n = value.shape[0]
    for j in range(TILE_ROWS):
        ref[pl.ds(j, n, stride=TILE_ROWS), :] = value[:, j * LANES:(j + 1) * LANES]


def _load_token_tiles(ref, n, dtype):
    return jnp.concatenate(
        [ref[pl.ds(j, n, stride=TILE_ROWS), :].astype(dtype) for j in range(TILE_ROWS)], axis=1)


def _tile_copy(src, src_row, dst, dst_row, sem):
    rows = lambda r: pl.ds(pl.multiple_of(r * TILE_ROWS, TILE_ROWS), TILE_ROWS)
    return pltpu.make_async_copy(src.at[rows(src_row)], dst.at[rows(dst_row)], sem)


def _merge_kernel(x_ref, pa_ref, gb_ref, yb_ref, wpb_ref, wout_ref, g2_ref, wr_ref, br_ref,
                  x2_ref, xn2_ref, te_ref, gate_ref, rank_ref, cnt_ref, carry_ref):
    tm = x_ref.shape[0]

    @pl.when(pl.program_id(0) == 0)
    def _():
        carry_ref[...] = jnp.zeros(carry_ref.shape, F32)

    pb = jnp.dot(yb_ref[...], wpb_ref[...], preferred_element_type=F32)
    merged = pa_ref[...].astype(F32) + gb_ref[...].astype(F32) * pb
    x2 = x_ref[...] + jnp.dot(merged.astype(BF16), wout_ref[...], preferred_element_type=F32)
    x2_ref[...] = x2
    xn2 = _rms(x2) * g2_ref[...]
    _store_token_tiles(xn2_ref, xn2)

    x_hi = xn2.astype(BF16)
    x_lo = (xn2 - x_hi.astype(F32)).astype(BF16)
    hi_pass = jnp.dot(x_hi, wr_ref[...], preferred_element_type=F32)
    lo_pass = jnp.dot(x_lo, wr_ref[:, :LANES], preferred_element_type=F32)
    logits = hi_pass[:, :LANES] + (hi_pass[:, LANES:] + lo_pass) + br_ref[...]
    lane = lax.broadcasted_iota(I32, (tm, LANES), 1).astype(F32)
    work = jnp.where(lane < N_EXPERTS, logits, -jnp.inf)
    onehot = jnp.zeros((tm, LANES), F32)
    vals, idxs = [], []
    for _ in range(TOP_K):
        mx = jnp.max(work, axis=1, keepdims=True)
        ix = jnp.min(jnp.where(work == mx, lane, float(LANES)), axis=1, keepdims=True)
        hit = lane == ix
        onehot = onehot + jnp.where(hit, 1.0, 0.0)
        work = jnp.where(hit, -jnp.inf, work)
        vals.append(mx)
        idxs.append(ix)
    exps = [jnp.exp(v - vals[0]) for v in vals]
    denom = exps[0] + exps[1] + exps[2] + exps[3]

    r_io = lax.broadcasted_iota(I32, (tm, tm), 0)
    c_io = lax.broadcasted_iota(I32, (tm, tm), 1)
    lower = jnp.where(c_io <= r_io, 1.0, 0.0).astype(BF16)
    pref = jnp.dot(lower, onehot.astype(BF16), preferred_element_type=F32)
    base = pref - onehot + carry_ref[...]
    te = jnp.zeros((tm, LANES), F32)
    gate = jnp.zeros((tm, LANES), F32)
    rank = jnp.zeros((tm, LANES), F32)
    for k in range(TOP_K):
        slot = lane == float(k)
        rk = jnp.sum(jnp.where(lane == idxs[k], base, 0.0), axis=1, keepdims=True)
        te = jnp.where(slot, idxs[k], te)
        gate = jnp.where(slot, exps[k] / denom, gate)
        rank = jnp.where(slot, rk, rank)
    te_ref[...] = te.astype(I32)
    gate_ref[...] = gate
    rank_ref[...] = rank.astype(I32)
    carry_ref[...] = carry_ref[...] + pref[tm - 1:tm, :]
    cnt_ref[...] = carry_ref[...]


def _merge(x2d, pa, gb, yb, wpb, wout, g2, wr, br):
    t, d = x2d.shape
    tm = min(TM_MERGE, t)
    row = lambda w: pl.BlockSpec((tm, w), lambda i: (i, 0))
    weights = (wpb, wout, g2, wr, br)
    out_shape = (
        jax.ShapeDtypeStruct((t, d), F32),
        jax.ShapeDtypeStruct((t * TILE_ROWS, LANES), F32),
        jax.ShapeDtypeStruct((t, LANES), I32),
        jax.ShapeDtypeStruct((t, LANES), F32),
        jax.ShapeDtypeStruct((t, LANES), I32),
        jax.ShapeDtypeStruct((1, LANES), F32),
    )
    return pl.pallas_call(
        _merge_kernel,
        grid=(t // tm,),
        in_specs=[row(d), row(d), row(d), row(ATTN_WIDTH)] + [_full_spec(w) for w in weights],
        out_specs=(row(d), pl.BlockSpec((tm * TILE_ROWS, LANES), lambda i: (i, 0)),
                   row(LANES), row(LANES), row(LANES),
                   pl.BlockSpec((1, LANES), lambda i: (0, 0))),
        out_shape=out_shape,
        scratch_shapes=[pltpu.VMEM((1, LANES), F32)],
        compiler_params=pltpu.CompilerParams(
            dimension_semantics=("arbitrary",), vmem_limit_bytes=VMEM_LIMIT_BYTES),
        name="merge",
    )(x2d, pa, gb, yb, *weights)


def _dispatch_kernel(pstart_ref, cnt_ref, nused_ref, te_ref, rank_ref, xn2_ref, xs_ref,
                     zero_ref, sem, zsem):
    tm = xn2_ref.shape[0] // TILE_ROWS
    blk_rows = MOE_BLOCK * TILE_ROWS
    n_blocks = xs_ref.shape[0] // blk_rows

    @pl.when(pl.program_id(0) == 0)
    def _():
        zero_ref[...] = jnp.zeros(zero_ref.shape, F32)

        def pad_expert(e, carry):
            cnt = cnt_ref[e]
            n_pad = (cnt + MOE_BLOCK - 1) // MOE_BLOCK * MOE_BLOCK - cnt
            first = pstart_ref[e] + cnt

            def start(r, c):
                _tile_copy(zero_ref, 0, xs_ref, first + r, zsem).start()
                return c

            def wait(r, c):
                _tile_copy(zero_ref, 0, xs_ref, 0, zsem).wait()
                return c

            lax.fori_loop(0, n_pad, start, 0)
            lax.fori_loop(0, n_pad, wait, 0)
            return carry

        lax.fori_loop(0, N_EXPERTS, pad_expert, 0)

        def zero_block(b, carry):
            cp = pltpu.make_async_copy(
                zero_ref, xs_ref.at[pl.ds(pl.multiple_of(b * blk_rows, blk_rows), blk_rows)], zsem)
            cp.start()
            cp.wait()
            return carry

        lax.fori_loop(nused_ref[0], n_blocks, zero_block, 0)

    def issue(t, carry):
        for k in range(TOP_K):
            dest = pstart_ref[te_ref[t * TOP_K + k]] + rank_ref[t * TOP_K + k]
            _tile_copy(xn2_ref, t, xs_ref, dest, sem).start()
        return carry

    lax.fori_loop(0, tm, issue, 0, unroll=ISSUE_UNROLL)

    def drain(t, carry):
        for k in range(TOP_K):
            _tile_copy(xn2_ref, 0, xs_ref, 0, sem).wait()
        return carry

    lax.fori_loop(0, tm, drain, 0)


def _dispatch(pstart, counts, n_used, te_flat, rank_flat, xn2_tiles, n_rows):
    t = xn2_tiles.shape[0] // TILE_ROWS
    tm = min(TM_ROWS, t)
    smem_blk = pl.BlockSpec((tm * TOP_K,), lambda i, *_: (i,), memory_space=pltpu.SMEM)
    grid_spec = pltpu.PrefetchScalarGridSpec(
        num_scalar_prefetch=3,
        grid=(t // tm,),
        in_specs=[smem_blk, smem_blk,
                  pl.BlockSpec((tm * TILE_ROWS, LANES), lambda i, *_: (i, 0))],
        out_specs=pl.BlockSpec(memory_space=pl.ANY),
        scratch_shapes=[pltpu.VMEM((MOE_BLOCK * TILE_ROWS, LANES), F32),
                        pltpu.SemaphoreType.DMA(()), pltpu.SemaphoreType.DMA(())],
    )
    return pl.pallas_call(
        _dispatch_kernel,
        grid_spec=grid_spec,
        out_shape=jax.ShapeDtypeStruct((n_rows * TILE_ROWS, LANES), F32),
        compiler_params=pltpu.CompilerParams(dimension_semantics=("arbitrary",)),
        name="dispatch",
    )(pstart, counts, n_used, te_flat, rank_flat, xn2_tiles)


def _moe_kernel(blk_e_ref, xs_ref, w1_ref, b1_ref, w2_ref, b2_ref, y_ref, w1b_ref, w2b_ref):
    i = pl.program_id(0)
    prev_e = blk_e_ref[jnp.maximum(i - 1, 0)]

    @pl.when(jnp.logical_or(i == 0, blk_e_ref[i] != prev_e))
    def _():
        def cast(src_ref, dst_ref):
            def body(r, c):
                rows = pl.ds(pl.multiple_of(r * LANES, LANES), LANES)
                dst_ref[rows, :] = src_ref[0, rows, :].astype(BF16)
                return c
            lax.fori_loop(0, src_ref.shape[1] // LANES, body, 0)

        cast(w1_ref, w1b_ref)
        cast(w2_ref, w2b_ref)

    xb = _load_token_tiles(xs_ref, MOE_BLOCK, BF16)
    hb = jnp.dot(xb, w1b_ref[...], preferred_element_type=F32) + b1_ref[0]
    hg = jnp.minimum(hb[:, :D_EXPERT], SWIGLU_LIMIT)
    hu = jnp.clip(hb[:, D_EXPERT:], -SWIGLU_LIMIT, SWIGLU_LIMIT)
    act = (hu + 1.0) * (hg * jax.nn.sigmoid(SWIGLU_ALPHA * hg))
    y = jnp.dot(act.astype(BF16), w2b_ref[...], preferred_element_type=F32) + b2_ref[0]
    _store_token_tiles(y_ref, y)


def _moe(blk_e, xs, w1, b1, w2, b2):
    d = D_MODEL
    n_rows = xs.shape[0] // TILE_ROWS
    n_blocks = n_rows // MOE_BLOCK
    tile_blk = pl.BlockSpec((MOE_BLOCK * TILE_ROWS, LANES), lambda i, be: (i, 0))
    grid_spec = pltpu.PrefetchScalarGridSpec(
        num_scalar_prefetch=1,
        grid=(n_blocks,),
        in_specs=[
            tile_blk,
            pl.BlockSpec((1, d, 2 * D_EXPERT), lambda i, be: (be[i], 0, 0)),
            pl.BlockSpec((1, 1, 2 * D_EXPERT), lambda i, be: (be[i], 0, 0)),
            pl.BlockSpec((1, D_EXPERT, d), lambda i, be: (be[i], 0, 0)),
            pl.BlockSpec((1, 1, d), lambda i, be: (be[i], 0, 0)),
        ],
        out_specs=tile_blk,
        scratch_shapes=[pltpu.VMEM((d, 2 * D_EXPERT), BF16), pltpu.VMEM((D_EXPERT, d), BF16)],
    )
    return pl.pallas_call(
        _moe_kernel,
        grid_spec=grid_spec,
        out_shape=jax.ShapeDtypeStruct(xs.shape, F32),
        compiler_params=pltpu.CompilerParams(
            dimension_semantics=("arbitrary",), vmem_limit_bytes=VMEM_LIMIT_BYTES),
        name="moe",
    )(blk_e, xs, w1, b1, w2, b2)


def _combine_kernel(pstart_ref, te_ref, rank_ref, x2_ref, gate_ref, ys_ref, out_ref, buf_ref, sem):
    tm = x2_ref.shape[0]

    def issue(t, carry):
        for k in range(TOP_K):
            src = pstart_ref[te_ref[t * TOP_K + k]] + rank_ref[t * TOP_K + k]
            _tile_copy(ys_ref, src, buf_ref.at[k], t, sem).start()
        return carry

    lax.fori_loop(0, tm, issue, 0, unroll=ISSUE_UNROLL)

    def drain(t, carry):
        for k in range(TOP_K):
            _tile_copy(ys_ref, 0, buf_ref.at[k], 0, sem).wait()
        return carry

    lax.fori_loop(0, tm, drain, 0)

    gate = gate_ref[...]
    out = x2_ref[...]
    for k in range(TOP_K):
        out = out + gate[:, k:k + 1] * _load_token_tiles(buf_ref.at[k], tm, F32)
    out_ref[...] = out


def _combine(pstart, te_flat, rank_flat, x2, gate, ys):
    t, d = x2.shape
    tm = min(TM_ROWS, t)
    smem_blk = pl.BlockSpec((tm * TOP_K,), lambda i, ps: (i,), memory_space=pltpu.SMEM)
    grid_spec = pltpu.PrefetchScalarGridSpec(
        num_scalar_prefetch=1,
        grid=(t // tm,),
        in_specs=[smem_blk, smem_blk,
                  pl.BlockSpec((tm, d), lambda i, ps: (i, 0)),
                  pl.BlockSpec((tm, LANES), lambda i, ps: (i, 0)),
                  pl.BlockSpec(memory_space=pl.ANY)],
        out_specs=pl.BlockSpec((tm, d), lambda i, ps: (i, 0)),
        scratch_shapes=[pltpu.VMEM((TOP_K, tm * TILE_ROWS, LANES), F32),
                        pltpu.SemaphoreType.DMA(())],
    )
    return pl.pallas_call(
        _combine_kernel,
        grid_spec=grid_spec,
        out_shape=jax.ShapeDtypeStruct((t, d), F32),
        compiler_params=pltpu.CompilerParams(
            dimension_semantics=("arbitrary",), vmem_limit_bytes=VMEM_LIMIT_BYTES),
        name="combine",
    )(pstart, te_flat, rank_flat, x2, gate, ys)


def kernel(x, norm1_g, w_in, q_lat_g, w_uq, q_norm_g, kv_norm_g, w_uv, w_q_idx, k_idx_g,
           gmlp_v_g, w_spatial, b_spatial, w_proj_a, w_proj_b, w_out, norm2_g,
           w_router, b_router, w_exp1, b_exp1, w_exp2, b_exp2):
    batch, seq, d = x.shape
    t = batch * seq
    assert d == D_MODEL and seq % min(KEY_BLOCK, seq) == 0 and seq % Q_BLOCK == 0
    assert t % TM_INPROJ == 0 and t % TM_MERGE == 0 and (t * TOP_K) % MOE_BLOCK == 0
    x2d = x.reshape(t, d)
    row = lambda v: v.reshape(1, -1).astype(F32)

    wuv = w_in[:, OFF_UV:OFF_CQ].astype(BF16)
    wcq = w_in[:, OFF_CQ:OFF_CKV].astype(BF16)
    wmisc = jnp.pad(w_in[:, OFF_CKV:OFF_GATE], ((0, 0), (0, 2 * LANES - (OFF_GATE - OFF_CKV)))
                    ).astype(BF16)
    wga = w_in[:, OFF_GATE:OFF_GATE + D_MODEL].astype(BF16)
    wgb = w_in[:, OFF_GATE + D_MODEL:IN_WIDTH].astype(BF16)
    wuq = w_uq.reshape(Q_RANK, N_HEADS * KV_RANK).astype(BF16)
    wqi = jnp.pad(w_q_idx * (IDX_DIM ** -0.5), ((0, 0), (0, 0), (0, LANES - IDX_DIM))
                  ).reshape(Q_RANK, IDX_HEADS * LANES).astype(BF16)
    kidx_g = jnp.pad(k_idx_g, (0, LANES - IDX_DIM)).reshape(1, LANES).astype(F32)
    qn_g = row(q_norm_g) * (KV_RANK ** -0.5 * LOG2E)
    wuvp = jnp.zeros((N_HEADS, KV_RANK, ATTN_WIDTH), F32)
    for h in range(N_HEADS):
        wuvp = wuvp.at[h, :, h * HEAD_V:(h + 1) * HEAD_V].set(w_uv[h])
    wuvp = wuvp.astype(BF16)

    pa, gb, q_hm, qi_hm, kv, ki, wi = _inproj(
        x2d, row(norm1_g), wuv, wcq, wmisc, wga, wgb, row(q_lat_g), wuq, qn_g,
        row(kv_norm_g), wqi, kidx_g, row(gmlp_v_g), w_spatial.astype(F32),
        b_spatial.T.astype(F32), w_proj_a.astype(BF16))

    yb = _dsa(q_hm, qi_hm, wi, kv, ki, wuvp, batch, seq)

    wr = jnp.pad(w_router, ((0, 0), (0, LANES - N_EXPERTS))).astype(F32)
    wr_hi = wr.astype(BF16)
    wr = jnp.concatenate([wr_hi, (wr - wr_hi.astype(F32)).astype(BF16)], axis=1)
    br = jnp.pad(b_router, (0, LANES - N_EXPERTS)).reshape(1, LANES).astype(F32)
    x2, xn2, te, gate, rank, cnt = _merge(
        x2d, pa, gb, yb, w_proj_b.astype(BF16), w_out.astype(BF16), row(norm2_g), wr, br)

    counts = cnt[0, :N_EXPERTS].astype(I32)
    padded = (counts + MOE_BLOCK - 1) // MOE_BLOCK * MOE_BLOCK
    pad_end = jnp.cumsum(padded)
    pstart = (pad_end - padded).astype(I32)
    n_blocks = (t * TOP_K) // MOE_BLOCK + N_EXPERTS
    blk_start = jnp.arange(n_blocks, dtype=I32) * MOE_BLOCK
    blk_e = jnp.minimum(jnp.sum((pad_end[None, :] <= blk_start[:, None]).astype(I32), axis=1),
                        N_EXPERTS - 1).astype(I32)
    n_used = (pad_end[N_EXPERTS - 1:] // MOE_BLOCK).astype(I32)
    te_flat = te[:, :TOP_K].reshape(t * TOP_K)
    rank_flat = rank[:, :TOP_K].reshape(t * TOP_K)

    xs = _dispatch(pstart, counts, n_used, te_flat, rank_flat, xn2, n_blocks * MOE_BLOCK)
    ys = _moe(blk_e, xs, w_exp1.astype(F32), b_exp1.reshape(N_EXPERTS, 1, -1).astype(F32),
              w_exp2.astype(F32), b_exp2.reshape(N_EXPERTS, 1, -1).astype(F32))
    out = _combine(pstart, te_flat, rank_flat, x2, gate, ys)
    return out.reshape(batch, seq, d)
```

```python
import functools
import math

import jax
import jax.numpy as jnp
from jax import lax
from jax.experimental import pallas as pl
from jax.experimental.pallas import tpu as pltpu

F32 = jnp.float32
BF16 = jnp.bfloat16
I32 = jnp.int32

D_MODEL = 1024
GMLP_GROUPS = 4
GMLP_GROUP_DIM = 128
GMLP_WIDTH = GMLP_GROUPS * GMLP_GROUP_DIM
CHUNK = 128
N_HEADS = 8
Q_RANK = 256
KV_RANK = 128
HEAD_V = 64
ATTN_WIDTH = N_HEADS * HEAD_V
IDX_HEADS = 4
IDX_DIM = 64
TOPK_MAX = 256
N_EXPERTS = 32
TOP_K = 4
D_EXPERT = 1024
SWIGLU_LIMIT = 7.0
SWIGLU_ALPHA = 1.702
EPS = 1e-6

OFF_UV = 0
OFF_CQ = OFF_UV + 2 * GMLP_WIDTH
OFF_CKV = OFF_CQ + Q_RANK
OFF_KIDX = OFF_CKV + KV_RANK
OFF_WIDX = OFF_KIDX + IDX_DIM
OFF_GATE = OFF_WIDX + IDX_HEADS
IN_WIDTH = OFF_GATE + 2 * D_MODEL

LANES = 128
SUBLANES = 8
TILE_ROWS = D_MODEL // LANES
Q_BLOCK = 256
VMEM_LIMIT_BYTES = 56 * 1024 * 1024

TM_INPROJ = 512
KEY_BLOCK = 1024
COUNT_ROWS = 64
COUNT_BLOCK = 512
SOFTMAX_ROWS = 64
TM_MERGE = 512
TM_ROWS = 256
MOE_BLOCK = 512
ISSUE_UNROLL = 4
DMA_PRIORITIES = 2

INT_MIN = -(2 ** 31)
NEG_BIG = -1e30
LOG2E = 1.4426950408889634


def _full_spec(arr):
    nd = arr.ndim
    return pl.BlockSpec(arr.shape, lambda *_: (0,) * nd)


def _rms(v, axis=-1):
    return v * lax.rsqrt(jnp.mean(v * v, axis=axis, keepdims=True) + EPS)


def _gelu_tanh(v):
    c = math.sqrt(2.0 / math.pi)
    return 0.5 * v * (1.0 + jnp.tanh(c * (v + 0.044715 * (v * v * v))))


def _inproj_kernel(x_ref, g1_ref, wuv_ref, wcq_ref, wmisc_ref, wga_ref, wgb_ref,
                   qlat_g_ref, wuq_ref, qn_g_ref, kvn_g_ref, wqi_ref, kidx_g_ref,
                   vg_ref, wsp_ref, bsp_ref, wpa_ref,
                   pa_ref, gb_ref, q_ref, qi_ref, kv_ref, ki_ref, wi_ref,
                   ya_ref):
    tm = x_ref.shape[0]
    x = x_ref[...]
    xn = (_rms(x) * g1_ref[...]).astype(BF16)

    z = _gelu_tanh(jnp.dot(xn, wuv_ref[...], preferred_element_type=F32))
    r_io = lax.broadcasted_iota(I32, (CHUNK, CHUNK), 0)
    c_io = lax.broadcasted_iota(I32, (CHUNK, CHUNK), 1)
    causal = c_io <= r_io
    for g in range(GMLP_GROUPS):
        cols = slice(g * GMLP_GROUP_DIM, (g + 1) * GMLP_GROUP_DIM)
        u_g = z[:, cols]
        v_g = z[:, GMLP_WIDTH + g * GMLP_GROUP_DIM:GMLP_WIDTH + (g + 1) * GMLP_GROUP_DIM]
        vn_g = (_rms(v_g) * vg_ref[...]).astype(BF16)
        w_g = jnp.where(causal, wsp_ref[g], 0.0).astype(BF16)
        b_g = bsp_ref[:, g:g + 1]
        for c in range(tm // CHUNK):
            rows = slice(c * CHUNK, (c + 1) * CHUNK)
            sv = jnp.dot(w_g, vn_g[rows], preferred_element_type=F32) + b_g
            ya_ref[rows, cols] = (u_g[rows] * sv).astype(BF16)
    gate_a = jax.nn.sigmoid(jnp.dot(xn, wga_ref[...], preferred_element_type=F32))
    pa = gate_a * jnp.dot(ya_ref[...], wpa_ref[...], preferred_element_type=F32)
    pa_ref[...] = pa.astype(BF16)
    gb_ref[...] = jax.nn.sigmoid(
        jnp.dot(xn, wgb_ref[...], preferred_element_type=F32)).astype(BF16)

    c_q = (_rms(jnp.dot(xn, wcq_ref[...], preferred_element_type=F32))
           * qlat_g_ref[...]).astype(BF16)
    qf = jnp.dot(c_q, wuq_ref[...], preferred_element_type=F32)
    qif = jnp.dot(c_q, wqi_ref[...], preferred_element_type=F32)
    for h in range(N_HEADS):
        qh = (_rms(qf[:, h * KV_RANK:(h + 1) * KV_RANK]) * qn_g_ref[...]).astype(BF16)
        for j in range(tm // Q_BLOCK):
            q_ref[j, h] = qh[j * Q_BLOCK:(j + 1) * Q_BLOCK]
    for h in range(IDX_HEADS):
        qih = qif[:, h * LANES:(h + 1) * LANES].astype(BF16)
        for j in range(tm // Q_BLOCK):
            qi_ref[j, h] = qih[j * Q_BLOCK:(j + 1) * Q_BLOCK]

    hm = jnp.dot(xn, wmisc_ref[...], preferred_element_type=F32)
    hk = hm[:, KV_RANK:]
    lane = lax.broadcasted_iota(I32, hk.shape, 1)
    kv_ref[:, :KV_RANK] = (_rms(hm[:, :KV_RANK]) * kvn_g_ref[...]).astype(BF16)
    kv_ref[:, KV_RANK:] = jnp.where(lane == 0, 1.0, 0.0).astype(BF16)
    is_k = lane < IDX_DIM
    ms_k = jnp.sum(jnp.where(is_k, hk * hk, 0.0), axis=-1, keepdims=True) * (1.0 / IDX_DIM)
    ki = jnp.where(is_k, hk * lax.rsqrt(ms_k + EPS) * kidx_g_ref[...], 0.0)
    ki_ref[...] = ki.astype(BF16)
    wi_ref[...] = hk * (IDX_HEADS ** -0.5)


def _inproj(x2d, g1, wuv, wcq, wmisc, wga, wgb, qlat_g, wuq, qn_g, kvn_g, wqi, kidx_g,
            vg, wsp, bsp_t, wpa):
    t, d = x2d.shape
    tm = min(TM_INPROJ, t)
    nqb = t // Q_BLOCK
    weights = (g1, wuv, wcq, wmisc, wga, wgb, qlat_g, wuq, qn_g, kvn_g, wqi, kidx_g,
               vg, wsp, bsp_t, wpa)
    out_shape = (
        jax.ShapeDtypeStruct((t, d), BF16),
        jax.ShapeDtypeStruct((t, d), BF16),
        jax.ShapeDtypeStruct((nqb, N_HEADS, Q_BLOCK, KV_RANK), BF16),
        jax.ShapeDtypeStruct((nqb, IDX_HEADS, Q_BLOCK, LANES), BF16),
        jax.ShapeDtypeStruct((t, 2 * KV_RANK), BF16),
        jax.ShapeDtypeStruct((t, LANES), BF16),
        jax.ShapeDtypeStruct((t, LANES), F32),
    )
    row = lambda w: pl.BlockSpec((tm, w), lambda i: (i, 0))
    out_specs = (
        row(d), row(d),
        pl.BlockSpec((tm // Q_BLOCK, N_HEADS, Q_BLOCK, KV_RANK), lambda i: (i, 0, 0, 0)),
        pl.BlockSpec((tm // Q_BLOCK, IDX_HEADS, Q_BLOCK, LANES), lambda i: (i, 0, 0, 0)),
        row(2 * KV_RANK), row(LANES), row(LANES),
    )
    return pl.pallas_call(
        _inproj_kernel,
        grid=(t // tm,),
        in_specs=[row(d)] + [_full_spec(w) for w in weights],
        out_specs=out_specs,
        out_shape=out_shape,
        scratch_shapes=[pltpu.VMEM((tm, GMLP_WIDTH), BF16)],
        compiler_params=pltpu.CompilerParams(
            dimension_semantics=("arbitrary",), vmem_limit_bytes=VMEM_LIMIT_BYTES),
        name="inproj",
    )(x2d, *weights)


def _dsa_kernel(q_ref, qi_ref, wi_ref, kv_ref, ki_ref, wuvp_ref, yb_ref,
                keys_ref, m_ref, alpha_ref, acc_ref, p_ref, bias_ref, s0_ref, s1_ref,
                thr_ref, need_ref, run_ref, keyst_ref,
                *, kb, k_top):
    qb = pl.program_id(1)
    q0 = qb * Q_BLOCK
    nkb = (q0 + Q_BLOCK + kb - 1) // kb
    nt = (((1,), (1,)), ((), ()))

    qidx = qi_ref[0].reshape(IDX_HEADS * Q_BLOCK, LANES)
    wslab = wi_ref[...]
    w_cols = [wslab[:, IDX_DIM + h:IDX_DIM + h + 1] for h in range(IDX_HEADS)]
    row_t = q0 + lax.broadcasted_iota(I32, (Q_BLOCK, 1), 0)

    def score_block(j, carry):
        kblk = ki_ref[0, pl.ds(pl.multiple_of(j * kb, kb), kb), :]
        lg = lax.dot_general(qidx, kblk, nt, preferred_element_type=F32)
        sc = jnp.zeros((Q_BLOCK, kb), F32)
        for h in range(IDX_HEADS):
            sc = sc + jnp.maximum(lg[h * Q_BLOCK:(h + 1) * Q_BLOCK], 0.0) * w_cols[h]
        sc = jnp.where(sc == 0.0, 0.0, sc)
        bits = pltpu.bitcast(sc, I32)
        key = bits ^ ((bits >> 31) & 0x7FFFFFFF)
        kpos = j * kb + lax.broadcasted_iota(I32, (1, kb), 1)
        key = jnp.where(kpos <= row_t, key, INT_MIN)
        keys_ref[j] = key
        key_t = key.T
        for s in range(kb // COUNT_BLOCK):
            keyst_ref[j * (kb // COUNT_BLOCK) + s] = key_t[s * COUNT_BLOCK:(s + 1) * COUNT_BLOCK]
        return carry

    lax.fori_loop(0, nkb, score_block, 0)

    def count(cmp, thr):
        thr_b = jnp.broadcast_to(thr, (COUNT_ROWS, Q_BLOCK))

        def body(j, acc):
            for c in range(COUNT_BLOCK // COUNT_ROWS):
                k = keyst_ref[j, pl.ds(c * COUNT_ROWS, COUNT_ROWS), :]
                acc = acc + jnp.where(cmp(k, thr_b), 1.0, 0.0)
            return acc

        n_count = (q0 + Q_BLOCK + COUNT_BLOCK - 1) // COUNT_BLOCK
        acc = lax.fori_loop(0, n_count, body, jnp.zeros((COUNT_ROWS, Q_BLOCK), F32))
        return jnp.sum(acc, axis=0, keepdims=True)

    ge = lambda a, b: a >= b
    gt = lambda a, b: a > b

    def bit_step(i, thr):
        cand = thr + lax.shift_left(jnp.int32(1), 31 - i)
        return jnp.where(count(ge, cand) >= k_top, cand, thr)

    thr = lax.fori_loop(0, 32, bit_step, jnp.full((1, Q_BLOCK), INT_MIN, I32))

    thr = jnp.maximum(thr, INT_MIN + 1)
    has_ties = jnp.max(count(ge, thr)) > k_top
    to_rows = lambda v: jnp.broadcast_to(v, (LANES, Q_BLOCK)).T
    thr_ref[...] = to_rows(thr)

    @pl.when(has_ties)
    def _():
        need_ref[...] = to_rows(k_top - count(gt, thr))
        run_ref[...] = jnp.zeros(run_ref.shape, F32)

    def bias_ties(j):
        k = keys_ref[j]
        t = thr_ref[:, 0:1]
        eq = k == t
        r_io = lax.broadcasted_iota(I32, (kb, kb), 0)
        c_io = lax.broadcasted_iota(I32, (kb, kb), 1)
        upper = jnp.where(r_io <= c_io, 1.0, 0.0).astype(BF16)
        pref = jnp.dot(jnp.where(eq, 1.0, 0.0).astype(BF16), upper,
                       preferred_element_type=F32)
        run = run_ref[:, 0:1]
        sel = (k > t) | (eq & (pref + run <= need_ref[:, 0:1]))
        bias_ref[...] = jnp.where(sel, 0.0, NEG_BIG)
        run_ref[...] = jnp.broadcast_to(run + pref[:, kb - 1:kb], run_ref.shape)

    def bias_plain(j):
        for r in range(Q_BLOCK // SOFTMAX_ROWS):
            sub = pl.ds(r * SOFTMAX_ROWS, SOFTMAX_ROWS)
            t = thr_ref[sub]
            for c in range(kb // LANES):
                cols = pl.ds(c * LANES, LANES)
                bias_ref[sub, cols] = jnp.where(keys_ref[j, sub, cols] >= t, 0.0, NEG_BIG)

    m_ref[...] = jnp.full(m_ref.shape, NEG_BIG, F32)
    acc_ref[...] = jnp.zeros(acc_ref.shape, F32)
    s_refs = (s0_ref, s1_ref)
    lo, hi = pl.ds(0, KV_RANK), pl.ds(KV_RANK, KV_RANK)

    def key_rows(j):
        return pl.ds(pl.multiple_of(j * kb, kb), kb)

    def qk(h, keys):
        s_refs[h % 2][...] = lax.dot_general(q_ref[0, h], kv_ref[0, keys, lo], nt,
                                             preferred_element_type=F32)

    qk(0, key_rows(0))

    def attn_block(j, carry):
        keys_of_block = key_rows(j)
        lax.cond(has_ties, bias_ties, bias_plain, j)
        for h in range(N_HEADS):
            s_ref = s_refs[h % 2]
            if h + 1 < N_HEADS:
                qk(h + 1, keys_of_block)
            else:
                qk(0, key_rows(jnp.minimum(j + 1, nkb - 1)))
            for r in range(Q_BLOCK // SOFTMAX_ROWS):
                sub = pl.ds(r * SOFTMAX_ROWS, SOFTMAX_ROWS)
                rows = pl.ds(h * Q_BLOCK + r * SOFTMAX_ROWS, SOFTMAX_ROWS)
                s = s_ref[sub, :] + bias_ref[sub, :]
                s_ref[sub, :] = s
                m_old = m_ref[rows]
                m_new = jnp.maximum(m_old, jnp.max(s, axis=1, keepdims=True))
                alpha_ref[rows] = jnp.exp2(m_old - m_new)
                m_ref[rows] = m_new
                for c in range(kb // LANES):
                    cols = pl.ds(c * LANES, LANES)
                    p_ref[rows, cols] = jnp.exp2(s_ref[sub, cols] - m_new).astype(BF16)
            h_rows = pl.ds(h * Q_BLOCK, Q_BLOCK)
            pv = jnp.dot(p_ref[h_rows], kv_ref[0, keys_of_block, :],
                         preferred_element_type=F32)
            a = alpha_ref[h_rows]
            acc_ref[h_rows, lo] = a * acc_ref[h_rows, lo] + pv[:, :KV_RANK]
            acc_ref[h_rows, hi] = a * acc_ref[h_rows, hi] + pv[:, KV_RANK:]
        return carry

    lax.fori_loop(0, nkb, attn_block, 0)

    o = (acc_ref[:, pl.ds(0, KV_RANK)] / acc_ref[:, pl.ds(KV_RANK, 1)]).astype(BF16)
    y = jnp.zeros((Q_BLOCK, ATTN_WIDTH), F32)
    for h in range(N_HEADS):
        y = y + jnp.dot(o[h * Q_BLOCK:(h + 1) * Q_BLOCK], wuvp_ref[h],
                        preferred_element_type=F32)
    yb_ref[...] = y.astype(BF16)


def _dsa(q_hm, qi_hm, wi, kv, ki, wuvp, batch, seq):
    nq = seq // Q_BLOCK
    kb = min(KEY_BLOCK, seq)
    k_top = min(TOPK_MAX, seq // 4)
    rows = N_HEADS * Q_BLOCK
    kern = functools.partial(_dsa_kernel, kb=kb, k_top=float(k_top))
    return pl.pallas_call(
        kern,
        grid=(batch, nq),
        in_specs=[
            pl.BlockSpec((1, N_HEADS, Q_BLOCK, KV_RANK), lambda b, i: (b * nq + i, 0, 0, 0)),
            pl.BlockSpec((1, IDX_HEADS, Q_BLOCK, LANES), lambda b, i: (b * nq + i, 0, 0, 0)),
            pl.BlockSpec((Q_BLOCK, LANES), lambda b, i: (b * nq + i, 0)),
            pl.BlockSpec((1, seq, 2 * KV_RANK), lambda b, i: (b, 0, 0)),
            pl.BlockSpec((1, seq, LANES), lambda b, i: (b, 0, 0)),
            _full_spec(wuvp),
        ],
        out_specs=pl.BlockSpec((Q_BLOCK, ATTN_WIDTH), lambda b, i: (b * nq + i, 0)),
        out_shape=jax.ShapeDtypeStruct((batch * seq, ATTN_WIDTH), BF16),
        scratch_shapes=[
            pltpu.VMEM((seq // kb, Q_BLOCK, kb), I32),
            pltpu.VMEM((rows, LANES), F32),
            pltpu.VMEM((rows, LANES), F32),
            pltpu.VMEM((rows, 2 * KV_RANK), F32),
            pltpu.VMEM((rows, kb), BF16),
            pltpu.VMEM((Q_BLOCK, kb), F32),
            pltpu.VMEM((Q_BLOCK, kb), F32),
            pltpu.VMEM((Q_BLOCK, kb), F32),
            pltpu.VMEM((Q_BLOCK, LANES), I32),
            pltpu.VMEM((Q_BLOCK, LANES), F32),
            pltpu.VMEM((Q_BLOCK, LANES), F32),
            pltpu.VMEM((seq // COUNT_BLOCK, COUNT_BLOCK, Q_BLOCK), I32),
        ],
        compiler_params=pltpu.CompilerParams(
            dimension_semantics=("arbitrary", "arbitrary"), vmem_limit_bytes=VMEM_LIMIT_BYTES),
        name="dsa",
    )(q_hm, qi_hm, wi, kv.reshape(batch, seq, 2 * KV_RANK), ki.reshape(batch, seq, LANES), wuvp)


def _store_token_tiles(ref, value):
    n = value.shape[0]
    for j in range(TILE_ROWS):
        ref[pl.ds(j, n, stride=TILE_ROWS), :] = value[:, j * LANES:(j + 1) * LANES]


def _load_token_tiles(ref, n, dtype):
    return jnp.concatenate(
        [ref[pl.ds(j, n, stride=TILE_ROWS), :].astype(dtype) for j in range(TILE_ROWS)], axis=1)


def _tile_copy(src, src_row, dst, dst_row, sem):
    rows = lambda r: pl.ds(pl.multiple_of(r * TILE_ROWS, TILE_ROWS), TILE_ROWS)
    return pltpu.make_async_copy(src.at[rows(src_row)], dst.at[rows(dst_row)], sem)


def _merge_kernel(x_ref, pa_ref, gb_ref, yb_ref, wpb_ref, wout_ref, g2_ref, wr_ref, br_ref,
                  x2_ref, xn2_ref, te_ref, gate_ref, rank_ref, cnt_ref, carry_ref):
    tm = x_ref.shape[0]

    @pl.when(pl.program_id(0) == 0)
    def _():
        carry_ref[...] = jnp.zeros(carry_ref.shape, F32)

    pb = jnp.dot(yb_ref[...], wpb_ref[...], preferred_element_type=F32)
    merged = pa_ref[...].astype(F32) + gb_ref[...].astype(F32) * pb
    x2 = x_ref[...] + jnp.dot(merged.astype(BF16), wout_ref[...], preferred_element_type=F32)
    x2_ref[...] = x2
    xn2 = _rms(x2) * g2_ref[...]
    _store_token_tiles(xn2_ref, xn2)

    x_hi = xn2.astype(BF16)
    x_lo = (xn2 - x_hi.astype(F32)).astype(BF16)
    hi_pass = jnp.dot(x_hi, wr_ref[...], preferred_element_type=F32)
    lo_pass = jnp.dot(x_lo, wr_ref[:, :LANES], preferred_element_type=F32)
    logits = hi_pass[:, :LANES] + (hi_pass[:, LANES:] + lo_pass) + br_ref[...]
    lane = lax.broadcasted_iota(I32, (tm, LANES), 1).astype(F32)
    work = jnp.where(lane < N_EXPERTS, logits, -jnp.inf)
    onehot = jnp.zeros((tm, LANES), F32)
    vals, idxs = [], []
    for _ in range(TOP_K):
        mx = jnp.max(work, axis=1, keepdims=True)
        ix = jnp.min(jnp.where(work == mx, lane, float(LANES)), axis=1, keepdims=True)
        hit = lane == ix
        onehot = onehot + jnp.where(hit, 1.0, 0.0)
        work = jnp.where(hit, -jnp.inf, work)
        vals.append(mx)
        idxs.append(ix)
    exps = [jnp.exp(v - vals[0]) for v in vals]
    denom = exps[0] + exps[1] + exps[2] + exps[3]

    r_io = lax.broadcasted_iota(I32, (tm, tm), 0)
    c_io = lax.broadcasted_iota(I32, (tm, tm), 1)
    lower = jnp.where(c_io <= r_io, 1.0, 0.0).astype(BF16)
    pref = jnp.dot(lower, onehot.astype(BF16), preferred_element_type=F32)
    base = pref - onehot + carry_ref[...]
    te = jnp.zeros((tm, LANES), F32)
    gate = jnp.zeros((tm, LANES), F32)
    rank = jnp.zeros((tm, LANES), F32)
    for k in range(TOP_K):
        slot = lane == float(k)
        rk = jnp.sum(jnp.where(lane == idxs[k], base, 0.0), axis=1, keepdims=True)
        te = jnp.where(slot, idxs[k], te)
        gate = jnp.where(slot, exps[k] / denom, gate)
        rank = jnp.where(slot, rk, rank)
    te_ref[...] = te.astype(I32)
    gate_ref[...] = gate
    rank_ref[...] = rank.astype(I32)
    carry_ref[...] = carry_ref[...] + pref[tm - 1:tm, :]
    cnt_ref[...] = carry_ref[...]


def _merge(x2d, pa, gb, yb, wpb, wout, g2, wr, br):
    t, d = x2d.shape
    tm = min(TM_MERGE, t)
    row = lambda w: pl.BlockSpec((tm, w), lambda i: (i, 0))
    weights = (wpb, wout, g2, wr, br)
    out_shape = (
        jax.ShapeDtypeStruct((t, d), F32),
        jax.ShapeDtypeStruct((t * TILE_ROWS, LANES), F32),
        jax.ShapeDtypeStruct((t, LANES), I32),
        jax.ShapeDtypeStruct((t, LANES), F32),
        jax.ShapeDtypeStruct((t, LANES), I32),
        jax.ShapeDtypeStruct((1, LANES), F32),
    )
    return pl.pallas_call(
        _merge_kernel,
        grid=(t // tm,),
        in_specs=[row(d), row(d), row(d), row(ATTN_WIDTH)] + [_full_spec(w) for w in weights],
        out_specs=(row(d), pl.BlockSpec((tm * TILE_ROWS, LANES), lambda i: (i, 0)),
                   row(LANES), row(LANES), row(LANES),
                   pl.BlockSpec((1, LANES), lambda i: (0, 0))),
        out_shape=out_shape,
        scratch_shapes=[pltpu.VMEM((1, LANES), F32)],
        compiler_params=pltpu.CompilerParams(
            dimension_semantics=("arbitrary",), vmem_limit_bytes=VMEM_LIMIT_BYTES),
        name="merge",
    )(x2d, pa, gb, yb, *weights)


def _dispatch_kernel(pstart_ref, cnt_ref, nused_ref, te_ref, rank_ref, xn2_ref, xs_ref,
                     zero_ref, sem, zsem):
    tm = xn2_ref.shape[0] // TILE_ROWS
    blk_rows = MOE_BLOCK * TILE_ROWS
    n_blocks = xs_ref.shape[0] // blk_rows

    @pl.when(pl.program_id(0) == 0)
    def _():
        zero_ref[...] = jnp.zeros(zero_ref.shape, F32)

        def pad_expert(e, carry):
            cnt = cnt_ref[e]
            n_pad = (cnt + MOE_BLOCK - 1) // MOE_BLOCK * MOE_BLOCK - cnt
            first = pstart_ref[e] + cnt

            def start(r, c):
                _tile_copy(zero_ref, 0, xs_ref, first + r, zsem).start()
                return c

            def wait(r, c):
                _tile_copy(zero_ref, 0, xs_ref, 0, zsem).wait()
                return c

            lax.fori_loop(0, n_pad, start, 0)
            lax.fori_loop(0, n_pad, wait, 0)
            return carry

        lax.fori_loop(0, N_EXPERTS, pad_expert, 0)

        def zero_block(b, carry):
            cp = pltpu.make_async_copy(
                zero_ref, xs_ref.at[pl.ds(pl.multiple_of(b * blk_rows, blk_rows), blk_rows)], zsem)
            cp.start()
            cp.wait()
            return carry

        lax.fori_loop(nused_ref[0], n_blocks, zero_block, 0)

    def issue(t, carry):
        for k in range(TOP_K):
            dest = pstart_ref[te_ref[t * TOP_K + k]] + rank_ref[t * TOP_K + k]
            _tile_copy(xn2_ref, t, xs_ref, dest, sem).start(priority=k % DMA_PRIORITIES)
        return carry

    lax.fori_loop(0, tm, issue, 0, unroll=ISSUE_UNROLL)

    def drain(t, carry):
        for k in range(TOP_K):
            _tile_copy(xn2_ref, 0, xs_ref, 0, sem).wait()
        return carry

    lax.fori_loop(0, tm, drain, 0)


def _dispatch(pstart, counts, n_used, te_flat, rank_flat, xn2_tiles, n_rows):
    t = xn2_tiles.shape[0] // TILE_ROWS
    tm = min(TM_ROWS, t)
    smem_blk = pl.BlockSpec((tm * TOP_K,), lambda i, *_: (i,), memory_space=pltpu.SMEM)
    grid_spec = pltpu.PrefetchScalarGridSpec(
        num_scalar_prefetch=3,
        grid=(t // tm,),
        in_specs=[smem_blk, smem_blk,
                  pl.BlockSpec((tm * TILE_ROWS, LANES), lambda i, *_: (i, 0))],
        out_specs=pl.BlockSpec(memory_space=pl.ANY),
        scratch_shapes=[pltpu.VMEM((MOE_BLOCK * TILE_ROWS, LANES), F32),
                        pltpu.SemaphoreType.DMA(()), pltpu.SemaphoreType.DMA(())],
    )
    return pl.pallas_call(
        _dispatch_kernel,
        grid_spec=grid_spec,
        out_shape=jax.ShapeDtypeStruct((n_rows * TILE_ROWS, LANES), F32),
        compiler_params=pltpu.CompilerParams(dimension_semantics=("arbitrary",)),
        name="dispatch",
    )(pstart, counts, n_used, te_flat, rank_flat, xn2_tiles)


def _moe_kernel(blk_e_ref, xs_ref, w1_ref, b1_ref, w2_ref, b2_ref, y_ref, w1b_ref, w2b_ref):
    i = pl.program_id(0)
    prev_e = blk_e_ref[jnp.maximum(i - 1, 0)]

    @pl.when(jnp.logical_or(i == 0, blk_e_ref[i] != prev_e))
    def _():
        def cast(src_ref, dst_ref):
            def body(r, c):
                rows = pl.ds(pl.multiple_of(r * LANES, LANES), LANES)
                dst_ref[rows, :] = src_ref[0, rows, :].astype(BF16)
                return c
            lax.fori_loop(0, src_ref.shape[1] // LANES, body, 0)

        cast(w1_ref, w1b_ref)
        cast(w2_ref, w2b_ref)

    xb = _load_token_tiles(xs_ref, MOE_BLOCK, BF16)
    hb = jnp.dot(xb, w1b_ref[...], preferred_element_type=F32) + b1_ref[0]
    hg = jnp.minimum(hb[:, :D_EXPERT], SWIGLU_LIMIT)
    hu = jnp.clip(hb[:, D_EXPERT:], -SWIGLU_LIMIT, SWIGLU_LIMIT)
    act = (hu + 1.0) * (hg * jax.nn.sigmoid(SWIGLU_ALPHA * hg))
    y = jnp.dot(act.astype(BF16), w2b_ref[...], preferred_element_type=F32) + b2_ref[0]
    _store_token_tiles(y_ref, y)


def _moe(blk_e, xs, w1, b1, w2, b2):
    d = D_MODEL
    n_rows = xs.shape[0] // TILE_ROWS
    n_blocks = n_rows // MOE_BLOCK
    tile_blk = pl.BlockSpec((MOE_BLOCK * TILE_ROWS, LANES), lambda i, be: (i, 0))
    grid_spec = pltpu.PrefetchScalarGridSpec(
        num_scalar_prefetch=1,
        grid=(n_blocks,),
        in_specs=[
            tile_blk,
            pl.BlockSpec((1, d, 2 * D_EXPERT), lambda i, be: (be[i], 0, 0)),
            pl.BlockSpec((1, 1, 2 * D_EXPERT), lambda i, be: (be[i], 0, 0)),
            pl.BlockSpec((1, D_EXPERT, d), lambda i, be: (be[i], 0, 0)),
            pl.BlockSpec((1, 1, d), lambda i, be: (be[i], 0, 0)),
        ],
        out_specs=tile_blk,
        scratch_shapes=[pltpu.VMEM((d, 2 * D_EXPERT), BF16), pltpu.VMEM((D_EXPERT, d), BF16)],
    )
    return pl.pallas_call(
        _moe_kernel,
        grid_spec=grid_spec,
        out_shape=jax.ShapeDtypeStruct(xs.shape, F32),
        compiler_params=pltpu.CompilerParams(
            dimension_semantics=("arbitrary",), vmem_limit_bytes=VMEM_LIMIT_BYTES),
        name="moe",
    )(blk_e, xs, w1, b1, w2, b2)


def _combine_kernel(pstart_ref, te_ref, rank_ref, x2_ref, gate_ref, ys_ref, out_ref, buf_ref, sem):
    tm = x2_ref.shape[0]

    def issue(t, carry):
        for k in range(TOP_K):
            src = pstart_ref[te_ref[t * TOP_K + k]] + rank_ref[t * TOP_K + k]
            _tile_copy(ys_ref, src, buf_ref.at[k], t, sem).start(priority=k % DMA_PRIORITIES)
        return carry

    lax.fori_loop(0, tm, issue, 0, unroll=ISSUE_UNROLL)

    def drain(t, carry):
        for k in range(TOP_K):
            _tile_copy(ys_ref, 0, buf_ref.at[k], 0, sem).wait()
        return carry

    lax.fori_loop(0, tm, drain, 0)

    gate = gate_ref[...]
    out = x2_ref[...]
    for k in range(TOP_K):
        out = out + gate[:, k:k + 1] * _load_token_tiles(buf_ref.at[k], tm, F32)
    out_ref[...] = out


def _combine(pstart, te_flat, rank_flat, x2, gate, ys):
    t, d = x2.shape
    tm = min(TM_ROWS, t)
    smem_blk = pl.BlockSpec((tm * TOP_K,), lambda i, ps: (i,), memory_space=pltpu.SMEM)
    grid_spec = pltpu.PrefetchScalarGridSpec(
        num_scalar_prefetch=1,
        grid=(t // tm,),
        in_specs=[smem_blk, smem_blk,
                  pl.BlockSpec((tm, d), lambda i, ps: (i, 0)),
                  pl.BlockSpec((tm, LANES), lambda i, ps: (i, 0)),
                  pl.BlockSpec(memory_space=pl.ANY)],
        out_specs=pl.BlockSpec((tm, d), lambda i, ps: (i, 0)),
        scratch_shapes=[pltpu.VMEM((TOP_K, tm * TILE_ROWS, LANES), F32),
                        pltpu.SemaphoreType.DMA(())],
    )
    return pl.pallas_call(
        _combine_kernel,
        grid_spec=grid_spec,
        out_shape=jax.ShapeDtypeStruct((t, d), F32),
        compiler_params=pltpu.CompilerParams(
            dimension_semantics=("arbitrary",), vmem_limit_bytes=VMEM_LIMIT_BYTES),
        name="combine",
    )(pstart, te_flat, rank_flat, x2, gate, ys)


def kernel(x, norm1_g, w_in, q_lat_g, w_uq, q_norm_g, kv_norm_g, w_uv, w_q_idx, k_idx_g,
           gmlp_v_g, w_spatial, b_spatial, w_proj_a, w_proj_b, w_out, norm2_g,
           w_router, b_router, w_exp1, b_exp1, w_exp2, b_exp2):
    batch, seq, d = x.shape
    t = batch * seq
    assert d == D_MODEL and seq % min(KEY_BLOCK, seq) == 0 and seq % Q_BLOCK == 0
    assert t % TM_INPROJ == 0 and t % TM_MERGE == 0 and (t * TOP_K) % MOE_BLOCK == 0
    x2d = x.reshape(t, d)
    row = lambda v: v.reshape(1, -1).astype(F32)

    wuv = w_in[:, OFF_UV:OFF_CQ].astype(BF16)
    wcq = w_in[:, OFF_CQ:OFF_CKV].astype(BF16)
    wmisc = jnp.pad(w_in[:, OFF_CKV:OFF_GATE], ((0, 0), (0, 2 * LANES - (OFF_GATE - OFF_CKV)))
                    ).astype(BF16)
    wga = w_in[:, OFF_GATE:OFF_GATE + D_MODEL].astype(BF16)
    wgb = w_in[:, OFF_GATE + D_MODEL:IN_WIDTH].astype(BF16)
    wuq = w_uq.reshape(Q_RANK, N_HEADS * KV_RANK).astype(BF16)
    wqi = jnp.pad(w_q_idx * (IDX_DIM ** -0.5), ((0, 0), (0, 0), (0, LANES - IDX_DIM))
                  ).reshape(Q_RANK, IDX_HEADS * LANES).astype(BF16)
    kidx_g = jnp.pad(k_idx_g, (0, LANES - IDX_DIM)).reshape(1, LANES).astype(F32)
    qn_g = row(q_norm_g) * (KV_RANK ** -0.5 * LOG2E)
    wuvp = jnp.zeros((N_HEADS, KV_RANK, ATTN_WIDTH), F32)
    for h in range(N_HEADS):
        wuvp = wuvp.at[h, :, h * HEAD_V:(h + 1) * HEAD_V].set(w_uv[h])
    wuvp = wuvp.astype(BF16)

    pa, gb, q_hm, qi_hm, kv, ki, wi = _inproj(
        x2d, row(norm1_g), wuv, wcq, wmisc, wga, wgb, row(q_lat_g), wuq, qn_g,
        row(kv_norm_g), wqi, kidx_g, row(gmlp_v_g), w_spatial.astype(F32),
        b_spatial.T.astype(F32), w_proj_a.astype(BF16))

    yb = _dsa(q_hm, qi_hm, wi, kv, ki, wuvp, batch, seq)

    wr = jnp.pad(w_router, ((0, 0), (0, LANES - N_EXPERTS))).astype(F32)
    wr_hi = wr.astype(BF16)
    wr = jnp.concatenate([wr_hi, (wr - wr_hi.astype(F32)).astype(BF16)], axis=1)
    br = jnp.pad(b_router, (0, LANES - N_EXPERTS)).reshape(1, LANES).astype(F32)
    x2, xn2, te, gate, rank, cnt = _merge(
        x2d, pa, gb, yb, w_proj_b.astype(BF16), w_out.astype(BF16), row(norm2_g), wr, br)

    counts = cnt[0, :N_EXPERTS].astype(I32)
    padded = (counts + MOE_BLOCK - 1) // MOE_BLOCK * MOE_BLOCK
    pad_end = jnp.cumsum(padded)
    pstart = (pad_end - padded).astype(I32)
    n_blocks = (t * TOP_K) // MOE_BLOCK + N_EXPERTS
    blk_start = jnp.arange(n_blocks, dtype=I32) * MOE_BLOCK
    blk_e = jnp.minimum(jnp.sum((pad_end[None, :] <= blk_start[:, None]).astype(I32), axis=1),
                        N_EXPERTS - 1).astype(I32)
    n_used = (pad_end[N_EXPERTS - 1:] // MOE_BLOCK).astype(I32)
    te_flat = te[:, :TOP_K].reshape(t * TOP_K)
    rank_flat = rank[:, :TOP_K].reshape(t * TOP_K)

    xs = _dispatch(pstart, counts, n_used, te_flat, rank_flat, xn2, n_blocks * MOE_BLOCK)
    ys = _moe(blk_e, xs, w_exp1.astype(F32), b_exp1.reshape(N_EXPERTS, 1, -1).astype(F32),
              w_exp2.astype(F32), b_exp2.reshape(N_EXPERTS, 1, -1).astype(F32))
    out = _combine(pstart, te_flat, rank_flat, x2, gate, ys)
    return out.reshape(batch, seq, d)
```

```python
import functools
import math

import jax
import jax.numpy as jnp
from jax import lax
from jax.experimental import pallas as pl
from jax.experimental.pallas import tpu as pltpu

F32 = jnp.float32
BF16 = jnp.bfloat16
I32 = jnp.int32

D_MODEL = 1024
GMLP_GROUPS = 4
GMLP_GROUP_DIM = 128
GMLP_WIDTH = GMLP_GROUPS * GMLP_GROUP_DIM
CHUNK = 128
N_HEADS = 8
Q_RANK = 256
KV_RANK = 128
HEAD_V = 64
ATTN_WIDTH = N_HEADS * HEAD_V
IDX_HEADS = 4
IDX_DIM = 64
TOPK_MAX = 256
N_EXPERTS = 32
TOP_K = 4
D_EXPERT = 1024
SWIGLU_LIMIT = 7.0
SWIGLU_ALPHA = 1.702
EPS = 1e-6

OFF_UV = 0
OFF_CQ = OFF_UV + 2 * GMLP_WIDTH
OFF_CKV = OFF_CQ + Q_RANK
OFF_KIDX = OFF_CKV + KV_RANK
OFF_WIDX = OFF_KIDX + IDX_DIM
OFF_GATE = OFF_WIDX + IDX_HEADS
IN_WIDTH = OFF_GATE + 2 * D_MODEL

LANES = 128
SUBLANES = 8
TILE_ROWS = D_MODEL // LANES
Q_BLOCK = 256
VMEM_LIMIT_BYTES = 56 * 1024 * 1024

TM_INPROJ = 1024
KEY_BLOCK = 1024
COUNT_ROWS = 64
COUNT_BLOCK = 512
SOFTMAX_ROWS = 64
TM_MERGE = 512
TM_ROWS = 256
MOE_BLOCK = 512
ISSUE_UNROLL = 4
DMA_PRIORITIES = 2

INT_MIN = -(2 ** 31)
NEG_BIG = -1e30
LOG2E = 1.4426950408889634


def _full_spec(arr):
    nd = arr.ndim
    return pl.BlockSpec(arr.shape, lambda *_: (0,) * nd)


def _rms(v, axis=-1):
    return v * lax.rsqrt(jnp.mean(v * v, axis=axis, keepdims=True) + EPS)


def _gelu_tanh(v):
    c = math.sqrt(2.0 / math.pi)
    return 0.5 * v * (1.0 + jnp.tanh(c * (v + 0.044715 * (v * v * v))))


def _inproj_kernel(x_ref, g1_ref, wuv_ref, wcq_ref, wmisc_ref, wga_ref, wgb_ref,
                   qlat_g_ref, wuq_ref, qn_g_ref, kvn_g_ref, wqi_ref, kidx_g_ref,
                   vg_ref, wsp_ref, bsp_ref, wpa_ref,
                   pa_ref, gb_ref, q_ref, qi_ref, kv_ref, ki_ref, wi_ref,
                   ya_ref):
    tm = x_ref.shape[0]
    x = x_ref[...]
    xn = (_rms(x) * g1_ref[...]).astype(BF16)

    z = _gelu_tanh(jnp.dot(xn, wuv_ref[...], preferred_element_type=F32))
    r_io = lax.broadcasted_iota(I32, (CHUNK, CHUNK), 0)
    c_io = lax.broadcasted_iota(I32, (CHUNK, CHUNK), 1)
    causal = c_io <= r_io
    for g in range(GMLP_GROUPS):
        cols = slice(g * GMLP_GROUP_DIM, (g + 1) * GMLP_GROUP_DIM)
        u_g = z[:, cols]
        v_g = z[:, GMLP_WIDTH + g * GMLP_GROUP_DIM:GMLP_WIDTH + (g + 1) * GMLP_GROUP_DIM]
        vn_g = (_rms(v_g) * vg_ref[...]).astype(BF16)
        w_g = jnp.where(causal, wsp_ref[g], 0.0).astype(BF16)
        b_g = bsp_ref[:, g:g + 1]
        for c in range(tm // CHUNK):
            rows = slice(c * CHUNK, (c + 1) * CHUNK)
            sv = jnp.dot(w_g, vn_g[rows], preferred_element_type=F32) + b_g
            ya_ref[rows, cols] = (u_g[rows] * sv).astype(BF16)
    gate_a = jax.nn.sigmoid(jnp.dot(xn, wga_ref[...], preferred_element_type=F32))
    pa = gate_a * jnp.dot(ya_ref[...], wpa_ref[...], preferred_element_type=F32)
    pa_ref[...] = pa.astype(BF16)
    gb_ref[...] = jax.nn.sigmoid(
        jnp.dot(xn, wgb_ref[...], preferred_element_type=F32)).astype(BF16)

    c_q = (_rms(jnp.dot(xn, wcq_ref[...], preferred_element_type=F32))
           * qlat_g_ref[...]).astype(BF16)
    qf = jnp.dot(c_q, wuq_ref[...], preferred_element_type=F32)
    qif = jnp.dot(c_q, wqi_ref[...], preferred_element_type=F32)
    for h in range(N_HEADS):
        qh = (_rms(qf[:, h * KV_RANK:(h + 1) * KV_RANK]) * qn_g_ref[...]).astype(BF16)
        for j in range(tm // Q_BLOCK):
            q_ref[j, h] = qh[j * Q_BLOCK:(j + 1) * Q_BLOCK]
    for h in range(IDX_HEADS):
        qih = qif[:, h * LANES:(h + 1) * LANES].astype(BF16)
        for j in range(tm // Q_BLOCK):
            qi_ref[j, h] = qih[j * Q_BLOCK:(j + 1) * Q_BLOCK]

    hm = jnp.dot(xn, wmisc_ref[...], preferred_element_type=F32)
    hk = hm[:, KV_RANK:]
    lane = lax.broadcasted_iota(I32, hk.shape, 1)
    kv_ref[:, :KV_RANK] = (_rms(hm[:, :KV_RANK]) * kvn_g_ref[...]).astype(BF16)
    kv_ref[:, KV_RANK:] = jnp.where(lane == 0, 1.0, 0.0).astype(BF16)
    is_k = lane < IDX_DIM
    ms_k = jnp.sum(jnp.where(is_k, hk * hk, 0.0), axis=-1, keepdims=True) * (1.0 / IDX_DIM)
    ki = jnp.where(is_k, hk * lax.rsqrt(ms_k + EPS) * kidx_g_ref[...], 0.0)
    ki_ref[...] = ki.astype(BF16)
    wi_ref[...] = hk * (IDX_HEADS ** -0.5)


def _inproj(x2d, g1, wuv, wcq, wmisc, wga, wgb, qlat_g, wuq, qn_g, kvn_g, wqi, kidx_g,
            vg, wsp, bsp_t, wpa):
    t, d = x2d.shape
    tm = min(TM_INPROJ, t)
    nqb = t // Q_BLOCK
    weights = (g1, wuv, wcq, wmisc, wga, wgb, qlat_g, wuq, qn_g, kvn_g, wqi, kidx_g,
               vg, wsp, bsp_t, wpa)
    out_shape = (
        jax.ShapeDtypeStruct((t, d), BF16),
        jax.ShapeDtypeStruct((t, d), BF16),
        jax.ShapeDtypeStruct((nqb, N_HEADS, Q_BLOCK, KV_RANK), BF16),
        jax.ShapeDtypeStruct((nqb, IDX_HEADS, Q_BLOCK, LANES), BF16),
        jax.ShapeDtypeStruct((t, 2 * KV_RANK), BF16),
        jax.ShapeDtypeStruct((t, LANES), BF16),
        jax.ShapeDtypeStruct((t, LANES), F32),
    )
    row = lambda w: pl.BlockSpec((tm, w), lambda i: (i, 0))
    out_specs = (
        row(d), row(d),
        pl.BlockSpec((tm // Q_BLOCK, N_HEADS, Q_BLOCK, KV_RANK), lambda i: (i, 0, 0, 0)),
        pl.BlockSpec((tm // Q_BLOCK, IDX_HEADS, Q_BLOCK, LANES), lambda i: (i, 0, 0, 0)),
        row(2 * KV_RANK), row(LANES), row(LANES),
    )
    return pl.pallas_call(
        _inproj_kernel,
        grid=(t // tm,),
        in_specs=[row(d)] + [_full_spec(w) for w in weights],
        out_specs=out_specs,
        out_shape=out_shape,
        scratch_shapes=[pltpu.VMEM((tm, GMLP_WIDTH), BF16)],
        compiler_params=pltpu.CompilerParams(
            dimension_semantics=("arbitrary",), vmem_limit_bytes=VMEM_LIMIT_BYTES),
        name="inproj",
    )(x2d, *weights)


def _dsa_kernel(q_ref, qi_ref, wi_ref, kv_ref, ki_ref, wuvp_ref, yb_ref,
                keys_ref, m_ref, alpha_ref, acc_ref, p_ref, bias_ref, s0_ref, s1_ref,
                thr_ref, need_ref, run_ref, keyst_ref,
                *, kb, cb, k_top):
    qb = pl.program_id(1)
    q0 = qb * Q_BLOCK
    nt = (((1,), (1,)), ((), ()))
    kb_half = kb // 2
    n_full = (q0 + Q_BLOCK) // kb
    rest = q0 + Q_BLOCK - n_full * kb
    nkb = n_full + jnp.where(rest > kb_half, 1, 0)
    has_half = jnp.logical_and(rest > 0, rest <= kb_half)

    qidx = qi_ref[0].reshape(IDX_HEADS * Q_BLOCK, LANES)
    wslab = wi_ref[...]
    w_cols = [wslab[:, IDX_DIM + h:IDX_DIM + h + 1] for h in range(IDX_HEADS)]
    row_t = q0 + lax.broadcasted_iota(I32, (Q_BLOCK, 1), 0)

    def score_block(j, width):
        kblk = ki_ref[0, pl.ds(pl.multiple_of(j * kb, kb_half), width), :]
        lg = lax.dot_general(qidx, kblk, nt, preferred_element_type=F32)
        sc = jnp.zeros((Q_BLOCK, width), F32)
        for h in range(IDX_HEADS):
            sc = sc + jnp.maximum(lg[h * Q_BLOCK:(h + 1) * Q_BLOCK], 0.0) * w_cols[h]
        sc = jnp.where(sc == 0.0, 0.0, sc)
        bits = pltpu.bitcast(sc, I32)
        key = bits ^ ((bits >> 31) & 0x7FFFFFFF)
        kpos = j * kb + lax.broadcasted_iota(I32, (1, width), 1)
        key = jnp.where(kpos <= row_t, key, INT_MIN)
        keys_ref[j, :, pl.ds(0, width)] = key
        key_t = key.T
        for s in range(width // cb):
            keyst_ref[j * (kb // cb) + s] = key_t[s * cb:(s + 1) * cb]

    def score_loop(j, carry):
        score_block(j, kb)
        return carry

    lax.fori_loop(0, nkb, score_loop, 0)

    @pl.when(has_half)
    def _():
        score_block(nkb, kb_half)

    def count(cmp, thr):
        thr_b = jnp.broadcast_to(thr, (COUNT_ROWS, Q_BLOCK))

        def body(j, acc):
            for c in range(cb // COUNT_ROWS):
                k = keyst_ref[j, pl.ds(c * COUNT_ROWS, COUNT_ROWS), :]
                acc = acc + jnp.where(cmp(k, thr_b), 1.0, 0.0)
            return acc

        n_count = (q0 + Q_BLOCK + cb - 1) // cb
        acc = lax.fori_loop(0, n_count, body, jnp.zeros((COUNT_ROWS, Q_BLOCK), F32))
        return jnp.sum(acc, axis=0, keepdims=True)

    ge = lambda a, b: a >= b
    gt = lambda a, b: a > b

    def bit_step(i, thr):
        cand = thr + lax.shift_left(jnp.int32(1), 31 - i)
        return jnp.where(count(ge, cand) >= k_top, cand, thr)

    thr = lax.fori_loop(0, 32, bit_step, jnp.full((1, Q_BLOCK), INT_MIN, I32))

    thr = jnp.maximum(thr, INT_MIN + 1)
    has_ties = jnp.max(count(ge, thr)) > k_top
    to_rows = lambda v: jnp.broadcast_to(v, (LANES, Q_BLOCK)).T
    thr_ref[...] = to_rows(thr)

    @pl.when(has_ties)
    def _():
        need_ref[...] = to_rows(k_top - count(gt, thr))
        run_ref[...] = jnp.zeros(run_ref.shape, F32)

    def bias_ties(j, width):
        r_io = lax.broadcasted_iota(I32, (LANES, LANES), 0)
        c_io = lax.broadcasted_iota(I32, (LANES, LANES), 1)
        upper = jnp.where(r_io <= c_io, 1.0, 0.0).astype(BF16)
        t = thr_ref[...]
        need = need_ref[...]
        run = run_ref[...]
        for c in range(width // LANES):
            cols = pl.ds(c * LANES, LANES)
            k = keys_ref[j, :, cols]
            eq = k == t
            pref = jnp.dot(jnp.where(eq, 1.0, 0.0).astype(BF16), upper,
                           preferred_element_type=F32)
            sel = (k > t) | (eq & (pref + run <= need))
            bias_ref[:, cols] = jnp.where(sel, 0.0, NEG_BIG)
            run = run + pref[:, LANES - 1:LANES]
        run_ref[...] = run

    def bias_plain(j, width):
        for r in range(Q_BLOCK // SOFTMAX_ROWS):
            sub = pl.ds(r * SOFTMAX_ROWS, SOFTMAX_ROWS)
            t = thr_ref[sub]
            for c in range(width // LANES):
                cols = pl.ds(c * LANES, LANES)
                bias_ref[sub, cols] = jnp.where(keys_ref[j, sub, cols] >= t, 0.0, NEG_BIG)

    m_ref[...] = jnp.full(m_ref.shape, NEG_BIG, F32)
    acc_ref[...] = jnp.zeros(acc_ref.shape, F32)
    s_refs = (s0_ref, s1_ref)
    lo, hi = pl.ds(0, KV_RANK), pl.ds(KV_RANK, KV_RANK)

    def key_rows(j, width):
        return pl.ds(pl.multiple_of(j * kb, kb_half), width)

    def qk(h, j, width):
        s_refs[h % 2][:, pl.ds(0, width)] = lax.dot_general(
            q_ref[0, h], kv_ref[0, key_rows(j, width), lo], nt, preferred_element_type=F32)

    def attend(j, width, next_j):
        lax.cond(has_ties, functools.partial(bias_ties, width=width),
                 functools.partial(bias_plain, width=width), j)
        wcols = pl.ds(0, width)
        for h in range(N_HEADS):
            s_ref = s_refs[h % 2]
            if h + 1 < N_HEADS:
                qk(h + 1, j, width)
            elif next_j is not None:
                qk(0, next_j, kb)
            for r in range(Q_BLOCK // SOFTMAX_ROWS):
                sub = pl.ds(r * SOFTMAX_ROWS, SOFTMAX_ROWS)
                rows = pl.ds(h * Q_BLOCK + r * SOFTMAX_ROWS, SOFTMAX_ROWS)
                s = s_ref[sub, wcols] + bias_ref[sub, wcols]
                s_ref[sub, wcols] = s
                m_old = m_ref[rows]
                m_new = jnp.maximum(m_old, jnp.max(s, axis=1, keepdims=True))
                alpha_ref[rows] = jnp.exp2(m_old - m_new)
                m_ref[rows] = m_new
                for c in range(width // LANES):
                    cols = pl.ds(c * LANES, LANES)
                    p_ref[rows, cols] = jnp.exp2(s_ref[sub, cols] - m_new).astype(BF16)
            h_rows = pl.ds(h * Q_BLOCK, Q_BLOCK)
            pv = jnp.dot(p_ref[h_rows, wcols], kv_ref[0, key_rows(j, width), :],
                         preferred_element_type=F32)
            a = alpha_ref[h_rows]
            acc_ref[h_rows, lo] = a * acc_ref[h_rows, lo] + pv[:, :KV_RANK]
            acc_ref[h_rows, hi] = a * acc_ref[h_rows, hi] + pv[:, KV_RANK:]

    qk(0, 0, kb)

    def attn_loop(j, carry):
        attend(j, kb, jnp.minimum(j + 1, jnp.maximum(nkb - 1, 0)))
        return carry

    lax.fori_loop(0, nkb, attn_loop, 0)

    @pl.when(has_half)
    def _():
        qk(0, nkb, kb_half)
        attend(nkb, kb_half, None)

    o = (acc_ref[:, pl.ds(0, KV_RANK)] / acc_ref[:, pl.ds(KV_RANK, 1)]).astype(BF16)
    y = jnp.zeros((Q_BLOCK, ATTN_WIDTH), F32)
    for h in range(N_HEADS):
        y = y + jnp.dot(o[h * Q_BLOCK:(h + 1) * Q_BLOCK], wuvp_ref[h],
                        preferred_element_type=F32)
    yb_ref[...] = y.astype(BF16)


def _dsa(q_hm, qi_hm, wi, kv, ki, wuvp, batch, seq):
    nq = seq // Q_BLOCK
    kb = min(KEY_BLOCK, seq)
    k_top = min(TOPK_MAX, seq // 4)
    rows = N_HEADS * Q_BLOCK
    cb = min(COUNT_BLOCK, kb // 2)
    kern = functools.partial(_dsa_kernel, kb=kb, cb=cb, k_top=float(k_top))
    return pl.pallas_call(
        kern,
        grid=(batch, nq),
        in_specs=[
            pl.BlockSpec((1, N_HEADS, Q_BLOCK, KV_RANK), lambda b, i: (b * nq + i, 0, 0, 0)),
            pl.BlockSpec((1, IDX_HEADS, Q_BLOCK, LANES), lambda b, i: (b * nq + i, 0, 0, 0)),
            pl.BlockSpec((Q_BLOCK, LANES), lambda b, i: (b * nq + i, 0)),
            pl.BlockSpec((1, seq, 2 * KV_RANK), lambda b, i: (b, 0, 0)),
            pl.BlockSpec((1, seq, LANES), lambda b, i: (b, 0, 0)),
            _full_spec(wuvp),
        ],
        out_specs=pl.BlockSpec((Q_BLOCK, ATTN_WIDTH), lambda b, i: (b * nq + i, 0)),
        out_shape=jax.ShapeDtypeStruct((batch * seq, ATTN_WIDTH), BF16),
        scratch_shapes=[
            pltpu.VMEM((seq // kb, Q_BLOCK, kb), I32),
            pltpu.VMEM((rows, LANES), F32),
            pltpu.VMEM((rows, LANES), F32),
            pltpu.VMEM((rows, 2 * KV_RANK), F32),
            pltpu.VMEM((rows, kb), BF16),
            pltpu.VMEM((Q_BLOCK, kb), F32),
            pltpu.VMEM((Q_BLOCK, kb), F32),
            pltpu.VMEM((Q_BLOCK, kb), F32),
            pltpu.VMEM((Q_BLOCK, LANES), I32),
            pltpu.VMEM((Q_BLOCK, LANES), F32),
            pltpu.VMEM((Q_BLOCK, LANES), F32),
            pltpu.VMEM((seq // cb, cb, Q_BLOCK), I32),
        ],
        compiler_params=pltpu.CompilerParams(
            dimension_semantics=("arbitrary", "arbitrary"), vmem_limit_bytes=VMEM_LIMIT_BYTES),
        name="dsa",
    )(q_hm, qi_hm, wi, kv.reshape(batch, seq, 2 * KV_RANK), ki.reshape(batch, seq, LANES), wuvp)


def _store_token_tiles(ref, value):
    n = value.shape[0]
    for j in range(TILE_ROWS):
        ref[pl.ds(j, n, stride=TILE_ROWS), :] = value[:, j * LANES:(j + 1) * LANES]


def _load_token_tiles(ref, n, dtype):
    return jnp.concatenate(
        [ref[pl.ds(j, n, stride=TILE_ROWS), :].astype(dtype) for j in range(TILE_ROWS)], axis=1)


def _tile_copy(src, src_row, dst, dst_row, sem):
    rows = lambda r: pl.ds(pl.multiple_of(r * TILE_ROWS, TILE_ROWS), TILE_ROWS)
    return pltpu.make_async_copy(src.at[rows(src_row)], dst.at[rows(dst_row)], sem)


def _merge_kernel(x_ref, pa_ref, gb_ref, yb_ref, wpb_ref, wout_ref, g2_ref, wr_ref, br_ref,
                  x2_ref, xn2_ref, te_ref, gate_ref, rank_ref, cnt_ref, carry_ref):
    tm = x_ref.shape[0]

    @pl.when(pl.program_id(0) == 0)
    def _():
        carry_ref[...] = jnp.zeros(carry_ref.shape, F32)

    pb = jnp.dot(yb_ref[...], wpb_ref[...], preferred_element_type=F32)
    merged = pa_ref[...].astype(F32) + gb_ref[...].astype(F32) * pb
    x2 = x_ref[...] + jnp.dot(merged.astype(BF16), wout_ref[...], preferred_element_type=F32)
    x2_ref[...] = x2
    xn2 = _rms(x2) * g2_ref[...]
    _store_token_tiles(xn2_ref, xn2)

    x_hi = xn2.astype(BF16)
    x_lo = (xn2 - x_hi.astype(F32)).astype(BF16)
    hi_pass = jnp.dot(x_hi, wr_ref[...], preferred_element_type=F32)
    lo_pass = jnp.dot(x_lo, wr_ref[:, :LANES], preferred_element_type=F32)
    logits = hi_pass[:, :LANES] + (hi_pass[:, LANES:] + lo_pass) + br_ref[...]
    lane = lax.broadcasted_iota(I32, (tm, LANES), 1).astype(F32)
    work = jnp.where(lane < N_EXPERTS, logits, -jnp.inf)
    onehot = jnp.zeros((tm, LANES), F32)
    vals, idxs = [], []
    for _ in range(TOP_K):
        mx = jnp.max(work, axis=1, keepdims=True)
        ix = jnp.min(jnp.where(work == mx, lane, float(LANES)), axis=1, keepdims=True)
        hit = lane == ix
        onehot = onehot + jnp.where(hit, 1.0, 0.0)
        work = jnp.where(hit, -jnp.inf, work)
        vals.append(mx)
        idxs.append(ix)
    exps = [jnp.exp(v - vals[0]) for v in vals]
    denom = exps[0] + exps[1] + exps[2] + exps[3]

    r_io = lax.broadcasted_iota(I32, (tm, tm), 0)
    c_io = lax.broadcasted_iota(I32, (tm, tm), 1)
    lower = jnp.where(c_io <= r_io, 1.0, 0.0).astype(BF16)
    pref = jnp.dot(lower, onehot.astype(BF16), preferred_element_type=F32)
    base = pref - onehot + carry_ref[...]
    te = jnp.zeros((tm, LANES), F32)
    gate = jnp.zeros((tm, LANES), F32)
    rank = jnp.zeros((tm, LANES), F32)
    for k in range(TOP_K):
        slot = lane == float(k)
        rk = jnp.sum(jnp.where(lane == idxs[k], base, 0.0), axis=1, keepdims=True)
        te = jnp.where(slot, idxs[k], te)
        gate = jnp.where(slot, exps[k] / denom, gate)
        rank = jnp.where(slot, rk, rank)
    te_ref[...] = te.astype(I32)
    gate_ref[...] = gate
    rank_ref[...] = rank.astype(I32)
    carry_ref[...] = carry_ref[...] + pref[tm - 1:tm, :]
    cnt_ref[...] = carry_ref[...]


def _merge(x2d, pa, gb, yb, wpb, wout, g2, wr, br):
    t, d = x2d.shape
    tm = min(TM_MERGE, t)
    row = lambda w: pl.BlockSpec((tm, w), lambda i: (i, 0))
    weights = (wpb, wout, g2, wr, br)
    out_shape = (
        jax.ShapeDtypeStruct((t, d), F32),
        jax.ShapeDtypeStruct((t * TILE_ROWS, LANES), F32),
        jax.ShapeDtypeStruct((t, LANES), I32),
        jax.ShapeDtypeStruct((t, LANES), F32),
        jax.ShapeDtypeStruct((t, LANES), I32),
        jax.ShapeDtypeStruct((1, LANES), F32),
    )
    return pl.pallas_call(
        _merge_kernel,
        grid=(t // tm,),
        in_specs=[row(d), row(d), row(d), row(ATTN_WIDTH)] + [_full_spec(w) for w in weights],
        out_specs=(row(d), pl.BlockSpec((tm * TILE_ROWS, LANES), lambda i: (i, 0)),
                   row(LANES), row(LANES), row(LANES),
                   pl.BlockSpec((1, LANES), lambda i: (0, 0))),
        out_shape=out_shape,
        scratch_shapes=[pltpu.VMEM((1, LANES), F32)],
        compiler_params=pltpu.CompilerParams(
            dimension_semantics=("arbitrary",), vmem_limit_bytes=VMEM_LIMIT_BYTES),
        name="merge",
    )(x2d, pa, gb, yb, *weights)


def _dispatch_kernel(pstart_ref, cnt_ref, nused_ref, te_ref, rank_ref, xn2_ref, xs_ref,
                     zero_ref, sem, zsem):
    tm = xn2_ref.shape[0] // TILE_ROWS
    blk_rows = MOE_BLOCK * TILE_ROWS
    n_blocks = xs_ref.shape[0] // blk_rows

    @pl.when(pl.program_id(0) == 0)
    def _():
        zero_ref[...] = jnp.zeros(zero_ref.shape, F32)

        def pad_expert(e, carry):
            cnt = cnt_ref[e]
            n_pad = (cnt + MOE_BLOCK - 1) // MOE_BLOCK * MOE_BLOCK - cnt
            first = pstart_ref[e] + cnt

            def start(r, c):
                _tile_copy(zero_ref, 0, xs_ref, first + r, zsem).start()
                return c

            def wait(r, c):
                _tile_copy(zero_ref, 0, xs_ref, 0, zsem).wait()
                return c

            lax.fori_loop(0, n_pad, start, 0)
            lax.fori_loop(0, n_pad, wait, 0)
            return carry

        lax.fori_loop(0, N_EXPERTS, pad_expert, 0)

        def zero_block(b, carry):
            cp = pltpu.make_async_copy(
                zero_ref, xs_ref.at[pl.ds(pl.multiple_of(b * blk_rows, blk_rows), blk_rows)], zsem)
            cp.start()
            cp.wait()
            return carry

        lax.fori_loop(nused_ref[0], n_blocks, zero_block, 0)

    def issue(t, carry):
        for k in range(TOP_K):
            dest = pstart_ref[te_ref[t * TOP_K + k]] + rank_ref[t * TOP_K + k]
            _tile_copy(xn2_ref, t, xs_ref, dest, sem).start(priority=k % DMA_PRIORITIES)
        return carry

    lax.fori_loop(0, tm, issue, 0, unroll=ISSUE_UNROLL)

    def drain(t, carry):
        for k in range(TOP_K):
            _tile_copy(xn2_ref, 0, xs_ref, 0, sem).wait()
        return carry

    lax.fori_loop(0, tm, drain, 0)


def _dispatch(pstart, counts, n_used, te_flat, rank_flat, xn2_tiles, n_rows):
    t = xn2_tiles.shape[0] // TILE_ROWS
    tm = min(TM_ROWS, t)
    smem_blk = pl.BlockSpec((tm * TOP_K,), lambda i, *_: (i,), memory_space=pltpu.SMEM)
    grid_spec = pltpu.PrefetchScalarGridSpec(
        num_scalar_prefetch=3,
        grid=(t // tm,),
        in_specs=[smem_blk, smem_blk,
                  pl.BlockSpec((tm * TILE_ROWS, LANES), lambda i, *_: (i, 0))],
        out_specs=pl.BlockSpec(memory_space=pl.ANY),
        scratch_shapes=[pltpu.VMEM((MOE_BLOCK * TILE_ROWS, LANES), F32),
                        pltpu.SemaphoreType.DMA(()), pltpu.SemaphoreType.DMA(())],
    )
    return pl.pallas_call(
        _dispatch_kernel,
        grid_spec=grid_spec,
        out_shape=jax.ShapeDtypeStruct((n_rows * TILE_ROWS, LANES), F32),
        compiler_params=pltpu.CompilerParams(dimension_semantics=("arbitrary",)),
        name="dispatch",
    )(pstart, counts, n_used, te_flat, rank_flat, xn2_tiles)


def _moe_kernel(blk_e_ref, nused_ref, xs_ref, w1_ref, b1_ref, w2_ref, b2_ref, y_ref,
                w1b_ref, w2b_ref):
    i = pl.program_id(0)

    @pl.when(i >= nused_ref[0])
    def _():
        y_ref[...] = jnp.zeros(y_ref.shape, F32)

    @pl.when(i < nused_ref[0])
    def _():
        _moe_block(i, blk_e_ref, xs_ref, w1_ref, b1_ref, w2_ref, b2_ref, y_ref, w1b_ref, w2b_ref)


def _moe_block(i, blk_e_ref, xs_ref, w1_ref, b1_ref, w2_ref, b2_ref, y_ref, w1b_ref, w2b_ref):
    prev_e = blk_e_ref[jnp.maximum(i - 1, 0)]

    @pl.when(jnp.logical_or(i == 0, blk_e_ref[i] != prev_e))
    def _():
        def cast(src_ref, dst_ref):
            def body(r, c):
                rows = pl.ds(pl.multiple_of(r * LANES, LANES), LANES)
                dst_ref[rows, :] = src_ref[0, rows, :].astype(BF16)
                return c
            lax.fori_loop(0, src_ref.shape[1] // LANES, body, 0)

        cast(w1_ref, w1b_ref)
        cast(w2_ref, w2b_ref)

    xb = _load_token_tiles(xs_ref, MOE_BLOCK, BF16)
    hb = jnp.dot(xb, w1b_ref[...], preferred_element_type=F32) + b1_ref[0]
    hg = jnp.minimum(hb[:, :D_EXPERT], SWIGLU_LIMIT)
    hu = jnp.clip(hb[:, D_EXPERT:], -SWIGLU_LIMIT, SWIGLU_LIMIT)
    act = (hu + 1.0) * (hg * jax.nn.sigmoid(SWIGLU_ALPHA * hg))
    y = jnp.dot(act.astype(BF16), w2b_ref[...], preferred_element_type=F32) + b2_ref[0]
    _store_token_tiles(y_ref, y)


def _moe(blk_e, n_used, xs, w1, b1, w2, b2):
    d = D_MODEL
    n_rows = xs.shape[0] // TILE_ROWS
    n_blocks = n_rows // MOE_BLOCK
    tile_blk = pl.BlockSpec((MOE_BLOCK * TILE_ROWS, LANES), lambda i, be, nu: (i, 0))
    grid_spec = pltpu.PrefetchScalarGridSpec(
        num_scalar_prefetch=2,
        grid=(n_blocks,),
        in_specs=[
            tile_blk,
            pl.BlockSpec((1, d, 2 * D_EXPERT), lambda i, be, nu: (be[i], 0, 0)),
            pl.BlockSpec((1, 1, 2 * D_EXPERT), lambda i, be, nu: (be[i], 0, 0)),
            pl.BlockSpec((1, D_EXPERT, d), lambda i, be, nu: (be[i], 0, 0)),
            pl.BlockSpec((1, 1, d), lambda i, be, nu: (be[i], 0, 0)),
        ],
        out_specs=tile_blk,
        scratch_shapes=[pltpu.VMEM((d, 2 * D_EXPERT), BF16), pltpu.VMEM((D_EXPERT, d), BF16)],
    )
    return pl.pallas_call(
        _moe_kernel,
        grid_spec=grid_spec,
        out_shape=jax.ShapeDtypeStruct(xs.shape, F32),
        compiler_params=pltpu.CompilerParams(
            dimension_semantics=("arbitrary",), vmem_limit_bytes=VMEM_LIMIT_BYTES),
        name="moe",
    )(blk_e, n_used, xs, w1, b1, w2, b2)


def _combine_kernel(pstart_ref, te_ref, rank_ref, x2_ref, gate_ref, ys_ref, out_ref, buf_ref, sem):
    tm = x2_ref.shape[0]

    def issue(t, carry):
        for k in range(TOP_K):
            src = pstart_ref[te_ref[t * TOP_K + k]] + rank_ref[t * TOP_K + k]
            _tile_copy(ys_ref, src, buf_ref.at[k], t, sem).start(priority=k % DMA_PRIORITIES)
        return carry

    lax.fori_loop(0, tm, issue, 0, unroll=ISSUE_UNROLL)

    def drain(t, carry):
        for k in range(TOP_K):
            _tile_copy(ys_ref, 0, buf_ref.at[k], 0, sem).wait()
        return carry

    lax.fori_loop(0, tm, drain, 0)

    gate = gate_ref[...]
    out = x2_ref[...]
    for k in range(TOP_K):
        out = out + gate[:, k:k + 1] * _load_token_tiles(buf_ref.at[k], tm, F32)
    out_ref[...] = out


def _combine(pstart, te_flat, rank_flat, x2, gate, ys):
    t, d = x2.shape
    tm = min(TM_ROWS, t)
    smem_blk = pl.BlockSpec((tm * TOP_K,), lambda i, ps: (i,), memory_space=pltpu.SMEM)
    grid_spec = pltpu.PrefetchScalarGridSpec(
        num_scalar_prefetch=1,
        grid=(t // tm,),
        in_specs=[smem_blk, smem_blk,
                  pl.BlockSpec((tm, d), lambda i, ps: (i, 0)),
                  pl.BlockSpec((tm, LANES), lambda i, ps: (i, 0)),
                  pl.BlockSpec(memory_space=pl.ANY)],
        out_specs=pl.BlockSpec((tm, d), lambda i, ps: (i, 0)),
        scratch_shapes=[pltpu.VMEM((TOP_K, tm * TILE_ROWS, LANES), F32),
                        pltpu.SemaphoreType.DMA(())],
    )
    return pl.pallas_call(
        _combine_kernel,
        grid_spec=grid_spec,
        out_shape=jax.ShapeDtypeStruct((t, d), F32),
        compiler_params=pltpu.CompilerParams(
            dimension_semantics=("arbitrary",), vmem_limit_bytes=VMEM_LIMIT_BYTES),
        name="combine",
    )(pstart, te_flat, rank_flat, x2, gate, ys)


def kernel(x, norm1_g, w_in, q_lat_g, w_uq, q_norm_g, kv_norm_g, w_uv, w_q_idx, k_idx_g,
           gmlp_v_g, w_spatial, b_spatial, w_proj_a, w_proj_b, w_out, norm2_g,
           w_router, b_router, w_exp1, b_exp1, w_exp2, b_exp2):
    batch, seq, d = x.shape
    t = batch * seq
    assert d == D_MODEL and seq % min(KEY_BLOCK, seq) == 0 and seq % Q_BLOCK == 0
    assert t % TM_INPROJ == 0 and t % TM_MERGE == 0 and (t * TOP_K) % MOE_BLOCK == 0
    x2d = x.reshape(t, d)
    row = lambda v: v.reshape(1, -1).astype(F32)

    wuv = w_in[:, OFF_UV:OFF_CQ].astype(BF16)
    wcq = w_in[:, OFF_CQ:OFF_CKV].astype(BF16)
    wmisc = jnp.pad(w_in[:, OFF_CKV:OFF_GATE], ((0, 0), (0, 2 * LANES - (OFF_GATE - OFF_CKV)))
                    ).astype(BF16)
    wga = w_in[:, OFF_GATE:OFF_GATE + D_MODEL].astype(BF16)
    wgb = w_in[:, OFF_GATE + D_MODEL:IN_WIDTH].astype(BF16)
    wuq = w_uq.reshape(Q_RANK, N_HEADS * KV_RANK).astype(BF16)
    wqi = jnp.pad(w_q_idx * (IDX_DIM ** -0.5), ((0, 0), (0, 0), (0, LANES - IDX_DIM))
                  ).reshape(Q_RANK, IDX_HEADS * LANES).astype(BF16)
    kidx_g = jnp.pad(k_idx_g, (0, LANES - IDX_DIM)).reshape(1, LANES).astype(F32)
    qn_g = row(q_norm_g) * (KV_RANK ** -0.5 * LOG2E)
    wuvp = jnp.zeros((N_HEADS, KV_RANK, ATTN_WIDTH), F32)
    for h in range(N_HEADS):
        wuvp = wuvp.at[h, :, h * HEAD_V:(h + 1) * HEAD_V].set(w_uv[h])
    wuvp = wuvp.astype(BF16)

    pa, gb, q_hm, qi_hm, kv, ki, wi = _inproj(
        x2d, row(norm1_g), wuv, wcq, wmisc, wga, wgb, row(q_lat_g), wuq, qn_g,
        row(kv_norm_g), wqi, kidx_g, row(gmlp_v_g), w_spatial.astype(F32),
        b_spatial.T.astype(F32), w_proj_a.astype(BF16))

    yb = _dsa(q_hm, qi_hm, wi, kv, ki, wuvp, batch, seq)

    wr = jnp.pad(w_router, ((0, 0), (0, LANES - N_EXPERTS))).astype(F32)
    wr_hi = wr.astype(BF16)
    wr = jnp.concatenate([wr_hi, (wr - wr_hi.astype(F32)).astype(BF16)], axis=1)
    br = jnp.pad(b_router, (0, LANES - N_EXPERTS)).reshape(1, LANES).astype(F32)
    x2, xn2, te, gate, rank, cnt = _merge(
        x2d, pa, gb, yb, w_proj_b.astype(BF16), w_out.astype(BF16), row(norm2_g), wr, br)

    counts = cnt[0, :N_EXPERTS].astype(I32)
    padded = (counts + MOE_BLOCK - 1) // MOE_BLOCK * MOE_BLOCK
    pad_end = jnp.cumsum(padded)
    pstart = (pad_end - padded).astype(I32)
    n_blocks = (t * TOP_K) // MOE_BLOCK + N_EXPERTS
    blk_start = jnp.arange(n_blocks, dtype=I32) * MOE_BLOCK
    blk_e = jnp.minimum(jnp.sum((pad_end[None, :] <= blk_start[:, None]).astype(I32), axis=1),
                        N_EXPERTS - 1).astype(I32)
    n_used = (pad_end[N_EXPERTS - 1:] // MOE_BLOCK).astype(I32)
    te_flat = te[:, :TOP_K].reshape(t * TOP_K)
    rank_flat = rank[:, :TOP_K].reshape(t * TOP_K)

    xs = _dispatch(pstart, counts, n_used, te_flat, rank_flat, xn2, n_blocks * MOE_BLOCK)
    ys = _moe(blk_e, n_used, xs, w_exp1.astype(F32), b_exp1.reshape(N_EXPERTS, 1, -1).astype(F32),
              w_exp2.astype(F32), b_exp2.reshape(N_EXPERTS, 1, -1).astype(F32))
    out = _combine(pstart, te_flat, rank_flat, x2, gate, ys)
    return out.reshape(batch, seq, d)
```

```python
import functools
import math

import jax
import jax.numpy as jnp
from jax import lax
from jax.experimental import pallas as pl
from jax.experimental.pallas import tpu as pltpu

F32 = jnp.float32
BF16 = jnp.bfloat16
I32 = jnp.int32

D_MODEL = 1024
GMLP_GROUPS = 4
GMLP_GROUP_DIM = 128
GMLP_WIDTH = GMLP_GROUPS * GMLP_GROUP_DIM
CHUNK = 128
N_HEADS = 8
Q_RANK = 256
KV_RANK = 128
HEAD_V = 64
ATTN_WIDTH = N_HEADS * HEAD_V
IDX_HEADS = 4
IDX_DIM = 64
TOPK_MAX = 256
N_EXPERTS = 32
TOP_K = 4
D_EXPERT = 1024
SWIGLU_LIMIT = 7.0
SWIGLU_ALPHA = 1.702
EPS = 1e-6

OFF_UV = 0
OFF_CQ = OFF_UV + 2 * GMLP_WIDTH
OFF_CKV = OFF_CQ + Q_RANK
OFF_KIDX = OFF_CKV + KV_RANK
OFF_WIDX = OFF_KIDX + IDX_DIM
OFF_GATE = OFF_WIDX + IDX_HEADS
IN_WIDTH = OFF_GATE + 2 * D_MODEL

LANES = 128
SUBLANES = 8
TILE_ROWS = D_MODEL // LANES
Q_BLOCK = 256
VMEM_LIMIT_BYTES = 56 * 1024 * 1024

TM_INPROJ = 1024
KEY_BLOCK = 1024
COUNT_ROWS = 64
COUNT_BLOCK = 512
SOFTMAX_ROWS = 64
TM_MERGE = 512
TM_ROWS = 256
MOE_BLOCK = 512
ISSUE_UNROLL = 8
DMA_PRIORITIES = 2

INT_MIN = -(2 ** 31)
NEG_BIG = -1e30
LOG2E = 1.4426950408889634


def _full_spec(arr):
    nd = arr.ndim
    return pl.BlockSpec(arr.shape, lambda *_: (0,) * nd)


def _rms(v, axis=-1):
    return v * lax.rsqrt(jnp.mean(v * v, axis=axis, keepdims=True) + EPS)


def _gelu_tanh(v):
    c = math.sqrt(2.0 / math.pi)
    return 0.5 * v * (1.0 + jnp.tanh(c * (v + 0.044715 * (v * v * v))))


def _inproj_kernel(x_ref, g1_ref, wuv_ref, wcq_ref, wmisc_ref, wga_ref, wgb_ref,
                   qlat_g_ref, wuq_ref, qn_g_ref, kvn_g_ref, wqi_ref, kidx_g_ref,
                   vg_ref, wsp_ref, bsp_ref, wpa_ref,
                   pa_ref, gb_ref, q_ref, qi_ref, kv_ref, ki_ref, wi_ref,
                   ya_ref):
    tm = x_ref.shape[0]
    x = x_ref[...]
    xn = (_rms(x) * g1_ref[...]).astype(BF16)

    z = _gelu_tanh(jnp.dot(xn, wuv_ref[...], preferred_element_type=F32))
    r_io = lax.broadcasted_iota(I32, (CHUNK, CHUNK), 0)
    c_io = lax.broadcasted_iota(I32, (CHUNK, CHUNK), 1)
    causal = c_io <= r_io
    for g in range(GMLP_GROUPS):
        cols = slice(g * GMLP_GROUP_DIM, (g + 1) * GMLP_GROUP_DIM)
        u_g = z[:, cols]
        v_g = z[:, GMLP_WIDTH + g * GMLP_GROUP_DIM:GMLP_WIDTH + (g + 1) * GMLP_GROUP_DIM]
        vn_g = (_rms(v_g) * vg_ref[...]).astype(BF16)
        w_g = jnp.where(causal, wsp_ref[g], 0.0).astype(BF16)
        b_g = bsp_ref[:, g:g + 1]
        for c in range(tm // CHUNK):
            rows = slice(c * CHUNK, (c + 1) * CHUNK)
            sv = jnp.dot(w_g, vn_g[rows], preferred_element_type=F32) + b_g
            ya_ref[rows, cols] = (u_g[rows] * sv).astype(BF16)
    gate_a = jax.nn.sigmoid(jnp.dot(xn, wga_ref[...], preferred_element_type=F32))
    pa = gate_a * jnp.dot(ya_ref[...], wpa_ref[...], preferred_element_type=F32)
    pa_ref[...] = pa.astype(BF16)
    gb_ref[...] = jax.nn.sigmoid(
        jnp.dot(xn, wgb_ref[...], preferred_element_type=F32)).astype(BF16)

    c_q = (_rms(jnp.dot(xn, wcq_ref[...], preferred_element_type=F32))
           * qlat_g_ref[...]).astype(BF16)
    qf = jnp.dot(c_q, wuq_ref[...], preferred_element_type=F32)
    qif = jnp.dot(c_q, wqi_ref[...], preferred_element_type=F32)
    for h in range(N_HEADS):
        qh = (_rms(qf[:, h * KV_RANK:(h + 1) * KV_RANK]) * qn_g_ref[...]).astype(BF16)
        for j in range(tm // Q_BLOCK):
            q_ref[j, h] = qh[j * Q_BLOCK:(j + 1) * Q_BLOCK]
    for h in range(IDX_HEADS):
        qih = qif[:, h * LANES:(h + 1) * LANES].astype(BF16)
        for j in range(tm // Q_BLOCK):
            qi_ref[j, h] = qih[j * Q_BLOCK:(j + 1) * Q_BLOCK]

    hm = jnp.dot(xn, wmisc_ref[...], preferred_element_type=F32)
    hk = hm[:, KV_RANK:]
    lane = lax.broadcasted_iota(I32, hk.shape, 1)
    kv_ref[:, :KV_RANK] = (_rms(hm[:, :KV_RANK]) * kvn_g_ref[...]).astype(BF16)
    kv_ref[:, KV_RANK:] = jnp.where(lane == 0, 1.0, 0.0).astype(BF16)
    is_k = lane < IDX_DIM
    ms_k = jnp.sum(jnp.where(is_k, hk * hk, 0.0), axis=-1, keepdims=True) * (1.0 / IDX_DIM)
    ki = jnp.where(is_k, hk * lax.rsqrt(ms_k + EPS) * kidx_g_ref[...], 0.0)
    ki_ref[...] = ki.astype(BF16)
    wi_ref[...] = hk * (IDX_HEADS ** -0.5)


def _inproj(x2d, g1, wuv, wcq, wmisc, wga, wgb, qlat_g, wuq, qn_g, kvn_g, wqi, kidx_g,
            vg, wsp, bsp_t, wpa):
    t, d = x2d.shape
    tm = min(TM_INPROJ, t)
    nqb = t // Q_BLOCK
    weights = (g1, wuv, wcq, wmisc, wga, wgb, qlat_g, wuq, qn_g, kvn_g, wqi, kidx_g,
               vg, wsp, bsp_t, wpa)
    out_shape = (
        jax.ShapeDtypeStruct((t, d), BF16),
        jax.ShapeDtypeStruct((t, d), BF16),
        jax.ShapeDtypeStruct((nqb, N_HEADS, Q_BLOCK, KV_RANK), BF16),
        jax.ShapeDtypeStruct((nqb, IDX_HEADS, Q_BLOCK, LANES), BF16),
        jax.ShapeDtypeStruct((t, 2 * KV_RANK), BF16),
        jax.ShapeDtypeStruct((t, LANES), BF16),
        jax.ShapeDtypeStruct((t, LANES), F32),
    )
    row = lambda w: pl.BlockSpec((tm, w), lambda i: (i, 0))
    out_specs = (
        row(d), row(d),
        pl.BlockSpec((tm // Q_BLOCK, N_HEADS, Q_BLOCK, KV_RANK), lambda i: (i, 0, 0, 0)),
        pl.BlockSpec((tm // Q_BLOCK, IDX_HEADS, Q_BLOCK, LANES), lambda i: (i, 0, 0, 0)),
        row(2 * KV_RANK), row(LANES), row(LANES),
    )
    return pl.pallas_call(
        _inproj_kernel,
        grid=(t // tm,),
        in_specs=[row(d)] + [_full_spec(w) for w in weights],
        out_specs=out_specs,
        out_shape=out_shape,
        scratch_shapes=[pltpu.VMEM((tm, GMLP_WIDTH), BF16)],
        compiler_params=pltpu.CompilerParams(
            dimension_semantics=("arbitrary",), vmem_limit_bytes=VMEM_LIMIT_BYTES),
        name="inproj",
    )(x2d, *weights)


def _dsa_kernel(q_ref, qi_ref, wi_ref, kv_ref, ki_ref, wuvp_ref, yb_ref,
                keys_ref, m_ref, alpha_ref, acc_ref, p_ref, bias_ref, s0_ref, s1_ref,
                thr_ref, need_ref, run_ref, keyst_ref,
                *, kb, cb, k_top):
    qb = pl.program_id(1)
    q0 = qb * Q_BLOCK
    nt = (((1,), (1,)), ((), ()))
    kb_half = kb // 2
    n_full = (q0 + Q_BLOCK) // kb
    rest = q0 + Q_BLOCK - n_full * kb
    nkb = n_full + jnp.where(rest > kb_half, 1, 0)
    has_half = jnp.logical_and(rest > 0, rest <= kb_half)

    qidx = qi_ref[0].reshape(IDX_HEADS * Q_BLOCK, LANES)
    wslab = wi_ref[...]
    w_cols = [wslab[:, IDX_DIM + h:IDX_DIM + h + 1] for h in range(IDX_HEADS)]
    row_t = q0 + lax.broadcasted_iota(I32, (Q_BLOCK, 1), 0)

    def score_block(j, width):
        kblk = ki_ref[0, pl.ds(pl.multiple_of(j * kb, kb_half), width), :]
        lg = lax.dot_general(qidx, kblk, nt, preferred_element_type=F32)
        sc = jnp.zeros((Q_BLOCK, width), F32)
        for h in range(IDX_HEADS):
            sc = sc + jnp.maximum(lg[h * Q_BLOCK:(h + 1) * Q_BLOCK], 0.0) * w_cols[h]
        sc = jnp.where(sc == 0.0, 0.0, sc)
        bits = pltpu.bitcast(sc, I32)
        key = bits ^ ((bits >> 31) & 0x7FFFFFFF)
        kpos = j * kb + lax.broadcasted_iota(I32, (1, width), 1)
        key = jnp.where(kpos <= row_t, key, INT_MIN)
        keys_ref[j, :, pl.ds(0, width)] = key
        key_t = key.T
        for s in range(width // cb):
            keyst_ref[j * (kb // cb) + s] = key_t[s * cb:(s + 1) * cb]

    def score_loop(j, carry):
        score_block(j, kb)
        return carry

    lax.fori_loop(0, nkb, score_loop, 0)

    @pl.when(has_half)
    def _():
        score_block(nkb, kb_half)

    def count(cmp, thr):
        thr_b = jnp.broadcast_to(thr, (COUNT_ROWS, Q_BLOCK))

        def body(j, acc):
            for c in range(cb // COUNT_ROWS):
                k = keyst_ref[j, pl.ds(c * COUNT_ROWS, COUNT_ROWS), :]
                acc = acc + jnp.where(cmp(k, thr_b), 1.0, 0.0)
            return acc

        n_count = (q0 + Q_BLOCK + cb - 1) // cb
        acc = lax.fori_loop(0, n_count, body, jnp.zeros((COUNT_ROWS, Q_BLOCK), F32))
        return jnp.sum(acc, axis=0, keepdims=True)

    ge = lambda a, b: a >= b
    gt = lambda a, b: a > b

    def bit_step(i, thr):
        cand = thr + lax.shift_left(jnp.int32(1), 31 - i)
        return jnp.where(count(ge, cand) >= k_top, cand, thr)

    thr = lax.fori_loop(0, 32, bit_step, jnp.full((1, Q_BLOCK), INT_MIN, I32))

    thr = jnp.maximum(thr, INT_MIN + 1)
    has_ties = jnp.max(count(ge, thr)) > k_top
    to_rows = lambda v: jnp.broadcast_to(v, (LANES, Q_BLOCK)).T
    thr_ref[...] = to_rows(thr)

    @pl.when(has_ties)
    def _():
        need_ref[...] = to_rows(k_top - count(gt, thr))
        run_ref[...] = jnp.zeros(run_ref.shape, F32)

    def bias_ties(j, width):
        r_io = lax.broadcasted_iota(I32, (LANES, LANES), 0)
        c_io = lax.broadcasted_iota(I32, (LANES, LANES), 1)
        upper = jnp.where(r_io <= c_io, 1.0, 0.0).astype(BF16)
        t = thr_ref[...]
        need = need_ref[...]
        run = run_ref[...]
        for c in range(width // LANES):
            cols = pl.ds(c * LANES, LANES)
            k = keys_ref[j, :, cols]
            eq = k == t
            pref = jnp.dot(jnp.where(eq, 1.0, 0.0).astype(BF16), upper,
                           preferred_element_type=F32)
            sel = (k > t) | (eq & (pref + run <= need))
            bias_ref[:, cols] = jnp.where(sel, 0.0, NEG_BIG)
            run = run + pref[:, LANES - 1:LANES]
        run_ref[...] = run

    def bias_plain(j, width):
        for r in range(Q_BLOCK // SOFTMAX_ROWS):
            sub = pl.ds(r * SOFTMAX_ROWS, SOFTMAX_ROWS)
            t = thr_ref[sub]
            for c in range(width // LANES):
                cols = pl.ds(c * LANES, LANES)
                bias_ref[sub, cols] = jnp.where(keys_ref[j, sub, cols] >= t, 0.0, NEG_BIG)

    m_ref[...] = jnp.full(m_ref.shape, NEG_BIG, F32)
    acc_ref[...] = jnp.zeros(acc_ref.shape, F32)
    s_refs = (s0_ref, s1_ref)
    lo, hi = pl.ds(0, KV_RANK), pl.ds(KV_RANK, KV_RANK)

    def key_rows(j, width):
        return pl.ds(pl.multiple_of(j * kb, kb_half), width)

    def qk(h, j, width):
        s_refs[h % 2][:, pl.ds(0, width)] = lax.dot_general(
            q_ref[0, h], kv_ref[0, key_rows(j, width), lo], nt, preferred_element_type=F32)

    def attend(j, width, next_j):
        lax.cond(has_ties, functools.partial(bias_ties, width=width),
                 functools.partial(bias_plain, width=width), j)
        wcols = pl.ds(0, width)
        for h in range(N_HEADS):
            s_ref = s_refs[h % 2]
            if h + 1 < N_HEADS:
                qk(h + 1, j, width)
            elif next_j is not None:
                qk(0, next_j, kb)
            for r in range(Q_BLOCK // SOFTMAX_ROWS):
                sub = pl.ds(r * SOFTMAX_ROWS, SOFTMAX_ROWS)
                rows = pl.ds(h * Q_BLOCK + r * SOFTMAX_ROWS, SOFTMAX_ROWS)
                s = s_ref[sub, wcols] + bias_ref[sub, wcols]
                s_ref[sub, wcols] = s
                m_old = m_ref[rows]
                m_new = jnp.maximum(m_old, jnp.max(s, axis=1, keepdims=True))
                alpha_ref[rows] = jnp.exp2(m_old - m_new)
                m_ref[rows] = m_new
                for c in range(width // LANES):
                    cols = pl.ds(c * LANES, LANES)
                    p_ref[rows, cols] = jnp.exp2(s_ref[sub, cols] - m_new).astype(BF16)
            h_rows = pl.ds(h * Q_BLOCK, Q_BLOCK)
            pv = jnp.dot(p_ref[h_rows, wcols], kv_ref[0, key_rows(j, width), :],
                         preferred_element_type=F32)
            a = alpha_ref[h_rows]
            acc_ref[h_rows, lo] = a * acc_ref[h_rows, lo] + pv[:, :KV_RANK]
            acc_ref[h_rows, hi] = a * acc_ref[h_rows, hi] + pv[:, KV_RANK:]

    qk(0, 0, kb)

    def attn_loop(j, carry):
        attend(j, kb, jnp.minimum(j + 1, jnp.maximum(nkb - 1, 0)))
        return carry

    lax.fori_loop(0, nkb, attn_loop, 0)

    @pl.when(has_half)
    def _():
        qk(0, nkb, kb_half)
        attend(nkb, kb_half, None)

    o = (acc_ref[:, pl.ds(0, KV_RANK)] / acc_ref[:, pl.ds(KV_RANK, 1)]).astype(BF16)
    y = jnp.zeros((Q_BLOCK, ATTN_WIDTH), F32)
    for h in range(N_HEADS):
        y = y + jnp.dot(o[h * Q_BLOCK:(h + 1) * Q_BLOCK], wuvp_ref[h],
                        preferred_element_type=F32)
    yb_ref[...] = y.astype(BF16)


def _dsa(q_hm, qi_hm, wi, kv, ki, wuvp, batch, seq):
    nq = seq // Q_BLOCK
    kb = min(KEY_BLOCK, seq)
    k_top = min(TOPK_MAX, seq // 4)
    rows = N_HEADS * Q_BLOCK
    cb = min(COUNT_BLOCK, kb // 2)
    kern = functools.partial(_dsa_kernel, kb=kb, cb=cb, k_top=float(k_top))
    return pl.pallas_call(
        kern,
        grid=(batch, nq),
        in_specs=[
            pl.BlockSpec((1, N_HEADS, Q_BLOCK, KV_RANK), lambda b, i: (b * nq + i, 0, 0, 0)),
            pl.BlockSpec((1, IDX_HEADS, Q_BLOCK, LANES), lambda b, i: (b * nq + i, 0, 0, 0)),
            pl.BlockSpec((Q_BLOCK, LANES), lambda b, i: (b * nq + i, 0)),
            pl.BlockSpec((1, seq, 2 * KV_RANK), lambda b, i: (b, 0, 0)),
            pl.BlockSpec((1, seq, LANES), lambda b, i: (b, 0, 0)),
            _full_spec(wuvp),
        ],
        out_specs=pl.BlockSpec((Q_BLOCK, ATTN_WIDTH), lambda b, i: (b * nq + i, 0)),
        out_shape=jax.ShapeDtypeStruct((batch * seq, ATTN_WIDTH), BF16),
        scratch_shapes=[
            pltpu.VMEM((seq // kb, Q_BLOCK, kb), I32),
            pltpu.VMEM((rows, LANES), F32),
            pltpu.VMEM((rows, LANES), F32),
            pltpu.VMEM((rows, 2 * KV_RANK), F32),
            pltpu.VMEM((rows, kb), BF16),
            pltpu.VMEM((Q_BLOCK, kb), F32),
            pltpu.VMEM((Q_BLOCK, kb), F32),
            pltpu.VMEM((Q_BLOCK, kb), F32),
            pltpu.VMEM((Q_BLOCK, LANES), I32),
            pltpu.VMEM((Q_BLOCK, LANES), F32),
            pltpu.VMEM((Q_BLOCK, LANES), F32),
            pltpu.VMEM((seq // cb, cb, Q_BLOCK), I32),
        ],
        compiler_params=pltpu.CompilerParams(
            dimension_semantics=("arbitrary", "arbitrary"), vmem_limit_bytes=VMEM_LIMIT_BYTES),
        name="dsa",
    )(q_hm, qi_hm, wi, kv.reshape(batch, seq, 2 * KV_RANK), ki.reshape(batch, seq, LANES), wuvp)


def _store_token_tiles(ref, value):
    n = value.shape[0]
    for j in range(TILE_ROWS):
        ref[pl.ds(j, n, stride=TILE_ROWS), :] = value[:, j * LANES:(j + 1) * LANES]


def _load_token_tiles(ref, n, dtype):
    return jnp.concatenate(
        [ref[pl.ds(j, n, stride=TILE_ROWS), :].astype(dtype) for j in range(TILE_ROWS)], axis=1)


def _tile_copy(src, src_row, dst, dst_row, sem):
    rows = lambda r: pl.ds(pl.multiple_of(r * TILE_ROWS, TILE_ROWS), TILE_ROWS)
    return pltpu.make_async_copy(src.at[rows(src_row)], dst.at[rows(dst_row)], sem)


def _merge_kernel(x_ref, pa_ref, gb_ref, yb_ref, wpb_ref, wout_ref, g2_ref, wr_ref, br_ref,
                  x2_ref, xn2_ref, te_ref, gate_ref, rank_ref, cnt_ref, carry_ref):
    tm = x_ref.shape[0]

    @pl.when(pl.program_id(0) == 0)
    def _():
        carry_ref[...] = jnp.zeros(carry_ref.shape, F32)

    pb = jnp.dot(yb_ref[...], wpb_ref[...], preferred_element_type=F32)
    merged = pa_ref[...].astype(F32) + gb_ref[...].astype(F32) * pb
    x2 = x_ref[...] + jnp.dot(merged.astype(BF16), wout_ref[...], preferred_element_type=F32)
    x2_ref[...] = x2
    xn2 = _rms(x2) * g2_ref[...]
    _store_token_tiles(xn2_ref, xn2)

    x_hi = xn2.astype(BF16)
    x_lo = (xn2 - x_hi.astype(F32)).astype(BF16)
    hi_pass = jnp.dot(x_hi, wr_ref[...], preferred_element_type=F32)
    lo_pass = jnp.dot(x_lo, wr_ref[:, :LANES], preferred_element_type=F32)
    logits = hi_pass[:, :LANES] + (hi_pass[:, LANES:] + lo_pass) + br_ref[...]
    lane = lax.broadcasted_iota(I32, (tm, LANES), 1).astype(F32)
    work = jnp.where(lane < N_EXPERTS, logits, -jnp.inf)
    onehot = jnp.zeros((tm, LANES), F32)
    vals, idxs = [], []
    for _ in range(TOP_K):
        mx = jnp.max(work, axis=1, keepdims=True)
        ix = jnp.min(jnp.where(work == mx, lane, float(LANES)), axis=1, keepdims=True)
        hit = lane == ix
        onehot = onehot + jnp.where(hit, 1.0, 0.0)
        work = jnp.where(hit, -jnp.inf, work)
        vals.append(mx)
        idxs.append(ix)
    exps = [jnp.exp(v - vals[0]) for v in vals]
    denom = exps[0] + exps[1] + exps[2] + exps[3]

    r_io = lax.broadcasted_iota(I32, (tm, tm), 0)
    c_io = lax.broadcasted_iota(I32, (tm, tm), 1)
    lower = jnp.where(c_io <= r_io, 1.0, 0.0).astype(BF16)
    pref = jnp.dot(lower, onehot.astype(BF16), preferred_element_type=F32)
    base = pref - onehot + carry_ref[...]
    te = jnp.zeros((tm, LANES), F32)
    gate = jnp.zeros((tm, LANES), F32)
    rank = jnp.zeros((tm, LANES), F32)
    for k in range(TOP_K):
        slot = lane == float(k)
        rk = jnp.sum(jnp.where(lane == idxs[k], base, 0.0), axis=1, keepdims=True)
        te = jnp.where(slot, idxs[k], te)
        gate = jnp.where(slot, exps[k] / denom, gate)
        rank = jnp.where(slot, rk, rank)
    te_ref[...] = te.astype(I32)
    gate_ref[...] = gate
    rank_ref[...] = rank.astype(I32)
    carry_ref[...] = carry_ref[...] + pref[tm - 1:tm, :]
    cnt_ref[...] = carry_ref[...]


def _merge(x2d, pa, gb, yb, wpb, wout, g2, wr, br):
    t, d = x2d.shape
    tm = min(TM_MERGE, t)
    row = lambda w: pl.BlockSpec((tm, w), lambda i: (i, 0))
    weights = (wpb, wout, g2, wr, br)
    out_shape = (
        jax.ShapeDtypeStruct((t, d), F32),
        jax.ShapeDtypeStruct((t * TILE_ROWS, LANES), F32),
        jax.ShapeDtypeStruct((t, LANES), I32),
        jax.ShapeDtypeStruct((t, LANES), F32),
        jax.ShapeDtypeStruct((t, LANES), I32),
        jax.ShapeDtypeStruct((1, LANES), F32),
    )
    return pl.pallas_call(
        _merge_kernel,
        grid=(t // tm,),
        in_specs=[row(d), row(d), row(d), row(ATTN_WIDTH)] + [_full_spec(w) for w in weights],
        out_specs=(row(d), pl.BlockSpec((tm * TILE_ROWS, LANES), lambda i: (i, 0)),
                   row(LANES), row(LANES), row(LANES),
                   pl.BlockSpec((1, LANES), lambda i: (0, 0))),
        out_shape=out_shape,
        scratch_shapes=[pltpu.VMEM((1, LANES), F32)],
        compiler_params=pltpu.CompilerParams(
            dimension_semantics=("arbitrary",), vmem_limit_bytes=VMEM_LIMIT_BYTES),
        name="merge",
    )(x2d, pa, gb, yb, *weights)


def _dispatch_kernel(pstart_ref, cnt_ref, nused_ref, te_ref, rank_ref, xn2_ref, xs_ref,
                     zero_ref, sem, zsem):
    tm = xn2_ref.shape[0] // TILE_ROWS
    blk_rows = MOE_BLOCK * TILE_ROWS
    n_blocks = xs_ref.shape[0] // blk_rows

    @pl.when(pl.program_id(0) == 0)
    def _():
        zero_ref[...] = jnp.zeros(zero_ref.shape, F32)

        def pad_expert(e, carry):
            cnt = cnt_ref[e]
            n_pad = (cnt + MOE_BLOCK - 1) // MOE_BLOCK * MOE_BLOCK - cnt
            first = pstart_ref[e] + cnt

            def start(r, c):
                _tile_copy(zero_ref, 0, xs_ref, first + r, zsem).start()
                return c

            def wait(r, c):
                _tile_copy(zero_ref, 0, xs_ref, 0, zsem).wait()
                return c

            lax.fori_loop(0, n_pad, start, 0)
            lax.fori_loop(0, n_pad, wait, 0)
            return carry

        lax.fori_loop(0, N_EXPERTS, pad_expert, 0)

        def zero_block(b, carry):
            cp = pltpu.make_async_copy(
                zero_ref, xs_ref.at[pl.ds(pl.multiple_of(b * blk_rows, blk_rows), blk_rows)], zsem)
            cp.start()
            cp.wait()
            return carry

        lax.fori_loop(nused_ref[0], n_blocks, zero_block, 0)

    def issue(t, carry):
        for k in range(TOP_K):
            dest = pstart_ref[te_ref[t * TOP_K + k]] + rank_ref[t * TOP_K + k]
            _tile_copy(xn2_ref, t, xs_ref, dest, sem).start(priority=k % DMA_PRIORITIES)
        return carry

    lax.fori_loop(0, tm, issue, 0, unroll=ISSUE_UNROLL)

    def drain(t, carry):
        for k in range(TOP_K):
            _tile_copy(xn2_ref, 0, xs_ref, 0, sem).wait()
        return carry

    lax.fori_loop(0, tm, drain, 0)


def _dispatch(pstart, counts, n_used, te_flat, rank_flat, xn2_tiles, n_rows):
    t = xn2_tiles.shape[0] // TILE_ROWS
    tm = min(TM_ROWS, t)
    smem_blk = pl.BlockSpec((tm * TOP_K,), lambda i, *_: (i,), memory_space=pltpu.SMEM)
    grid_spec = pltpu.PrefetchScalarGridSpec(
        num_scalar_prefetch=3,
        grid=(t // tm,),
        in_specs=[smem_blk, smem_blk,
                  pl.BlockSpec((tm * TILE_ROWS, LANES), lambda i, *_: (i, 0))],
        out_specs=pl.BlockSpec(memory_space=pl.ANY),
        scratch_shapes=[pltpu.VMEM((MOE_BLOCK * TILE_ROWS, LANES), F32),
                        pltpu.SemaphoreType.DMA(()), pltpu.SemaphoreType.DMA(())],
    )
    return pl.pallas_call(
        _dispatch_kernel,
        grid_spec=grid_spec,
        out_shape=jax.ShapeDtypeStruct((n_rows * TILE_ROWS, LANES), F32),
        compiler_params=pltpu.CompilerParams(dimension_semantics=("arbitrary",)),
        name="dispatch",
    )(pstart, counts, n_used, te_flat, rank_flat, xn2_tiles)


def _moe_kernel(blk_e_ref, nused_ref, xs_ref, w1_ref, b1_ref, w2_ref, b2_ref, y_ref,
                w1b_ref, w2b_ref):
    i = pl.program_id(0)

    @pl.when(i >= nused_ref[0])
    def _():
        y_ref[...] = jnp.zeros(y_ref.shape, F32)

    @pl.when(i < nused_ref[0])
    def _():
        _moe_block(i, blk_e_ref, xs_ref, w1_ref, b1_ref, w2_ref, b2_ref, y_ref, w1b_ref, w2b_ref)


def _moe_block(i, blk_e_ref, xs_ref, w1_ref, b1_ref, w2_ref, b2_ref, y_ref, w1b_ref, w2b_ref):
    prev_e = blk_e_ref[jnp.maximum(i - 1, 0)]

    @pl.when(jnp.logical_or(i == 0, blk_e_ref[i] != prev_e))
    def _():
        def cast(src_ref, dst_ref):
            def body(r, c):
                rows = pl.ds(pl.multiple_of(r * LANES, LANES), LANES)
                dst_ref[rows, :] = src_ref[0, rows, :].astype(BF16)
                return c
            lax.fori_loop(0, src_ref.shape[1] // LANES, body, 0)

        cast(w1_ref, w1b_ref)
        cast(w2_ref, w2b_ref)

    xb = _load_token_tiles(xs_ref, MOE_BLOCK, BF16)
    hb = jnp.dot(xb, w1b_ref[...], preferred_element_type=F32) + b1_ref[0]
    hg = jnp.minimum(hb[:, :D_EXPERT], SWIGLU_LIMIT)
    hu = jnp.clip(hb[:, D_EXPERT:], -SWIGLU_LIMIT, SWIGLU_LIMIT)
    act = (hu + 1.0) * (hg * jax.nn.sigmoid(SWIGLU_ALPHA * hg))
    y = jnp.dot(act.astype(BF16), w2b_ref[...], preferred_element_type=F32) + b2_ref[0]
    _store_token_tiles(y_ref, y)


def _moe(blk_e, n_used, xs, w1, b1, w2, b2):
    d = D_MODEL
    n_rows = xs.shape[0] // TILE_ROWS
    n_blocks = n_rows // MOE_BLOCK
    tile_blk = pl.BlockSpec((MOE_BLOCK * TILE_ROWS, LANES), lambda i, be, nu: (i, 0))
    grid_spec = pltpu.PrefetchScalarGridSpec(
        num_scalar_prefetch=2,
        grid=(n_blocks,),
        in_specs=[
            tile_blk,
            pl.BlockSpec((1, d, 2 * D_EXPERT), lambda i, be, nu: (be[i], 0, 0)),
            pl.BlockSpec((1, 1, 2 * D_EXPERT), lambda i, be, nu: (be[i], 0, 0)),
            pl.BlockSpec((1, D_EXPERT, d), lambda i, be, nu: (be[i], 0, 0)),
            pl.BlockSpec((1, 1, d), lambda i, be, nu: (be[i], 0, 0)),
        ],
        out_specs=tile_blk,
        scratch_shapes=[pltpu.VMEM((d, 2 * D_EXPERT), BF16), pltpu.VMEM((D_EXPERT, d), BF16)],
    )
    return pl.pallas_call(
        _moe_kernel,
        grid_spec=grid_spec,
        out_shape=jax.ShapeDtypeStruct(xs.shape, F32),
        compiler_params=pltpu.CompilerParams(
            dimension_semantics=("arbitrary",), vmem_limit_bytes=VMEM_LIMIT_BYTES),
        name="moe",
    )(blk_e, n_used, xs, w1, b1, w2, b2)


def _combine_kernel(pstart_ref, te_ref, rank_ref, x2_ref, gate_ref, ys_ref, out_ref, buf_ref, sem):
    tm = x2_ref.shape[0]

    def issue(t, carry):
        for k in range(TOP_K):
            src = pstart_ref[te_ref[t * TOP_K + k]] + rank_ref[t * TOP_K + k]
            _tile_copy(ys_ref, src, buf_ref.at[k], t, sem).start(priority=k % DMA_PRIORITIES)
        return carry

    lax.fori_loop(0, tm, issue, 0, unroll=ISSUE_UNROLL)

    def drain(t, carry):
        for k in range(TOP_K):
            _tile_copy(ys_ref, 0, buf_ref.at[k], 0, sem).wait()
        return carry

    lax.fori_loop(0, tm, drain, 0)

    gate = gate_ref[...]
    out = x2_ref[...]
    for k in range(TOP_K):
        out = out + gate[:, k:k + 1] * _load_token_tiles(buf_ref.at[k], tm, F32)
    out_ref[...] = out


def _combine(pstart, te_flat, rank_flat, x2, gate, ys):
    t, d = x2.shape
    tm = min(TM_ROWS, t)
    smem_blk = pl.BlockSpec((tm * TOP_K,), lambda i, ps: (i,), memory_space=pltpu.SMEM)
    grid_spec = pltpu.PrefetchScalarGridSpec(
        num_scalar_prefetch=1,
        grid=(t // tm,),
        in_specs=[smem_blk, smem_blk,
                  pl.BlockSpec((tm, d), lambda i, ps: (i, 0)),
                  pl.BlockSpec((tm, LANES), lambda i, ps: (i, 0)),
                  pl.BlockSpec(memory_space=pl.ANY)],
        out_specs=pl.BlockSpec((tm, d), lambda i, ps: (i, 0)),
        scratch_shapes=[pltpu.VMEM((TOP_K, tm * TILE_ROWS, LANES), F32),
                        pltpu.SemaphoreType.DMA(())],
    )
    return pl.pallas_call(
        _combine_kernel,
        grid_spec=grid_spec,
        out_shape=jax.ShapeDtypeStruct((t, d), F32),
        compiler_params=pltpu.CompilerParams(
            dimension_semantics=("arbitrary",), vmem_limit_bytes=VMEM_LIMIT_BYTES),
        name="combine",
    )(pstart, te_flat, rank_flat, x2, gate, ys)


def kernel(x, norm1_g, w_in, q_lat_g, w_uq, q_norm_g, kv_norm_g, w_uv, w_q_idx, k_idx_g,
           gmlp_v_g, w_spatial, b_spatial, w_proj_a, w_proj_b, w_out, norm2_g,
           w_router, b_router, w_exp1, b_exp1, w_exp2, b_exp2):
    batch, seq, d = x.shape
    t = batch * seq
    assert d == D_MODEL and seq % min(KEY_BLOCK, seq) == 0 and seq % Q_BLOCK == 0
    assert t % min(TM_INPROJ, t) == 0 and t % TM_MERGE == 0 and (t * TOP_K) % MOE_BLOCK == 0
    assert TILE_ROWS == SUBLANES, "a token tile must be exactly one (8, 128) f32 tile"
    x2d = x.reshape(t, d)
    row = lambda v: v.reshape(1, -1).astype(F32)

    wuv = w_in[:, OFF_UV:OFF_CQ].astype(BF16)
    wcq = w_in[:, OFF_CQ:OFF_CKV].astype(BF16)
    wmisc = jnp.pad(w_in[:, OFF_CKV:OFF_GATE], ((0, 0), (0, 2 * LANES - (OFF_GATE - OFF_CKV)))
                    ).astype(BF16)
    wga = w_in[:, OFF_GATE:OFF_GATE + D_MODEL].astype(BF16)
    wgb = w_in[:, OFF_GATE + D_MODEL:IN_WIDTH].astype(BF16)
    wuq = w_uq.reshape(Q_RANK, N_HEADS * KV_RANK).astype(BF16)
    wqi = jnp.pad(w_q_idx * (IDX_DIM ** -0.5), ((0, 0), (0, 0), (0, LANES - IDX_DIM))
                  ).reshape(Q_RANK, IDX_HEADS * LANES).astype(BF16)
    kidx_g = jnp.pad(k_idx_g, (0, LANES - IDX_DIM)).reshape(1, LANES).astype(F32)
    qn_g = row(q_norm_g) * (KV_RANK ** -0.5 * LOG2E)
    wuvp = jnp.zeros((N_HEADS, KV_RANK, ATTN_WIDTH), F32)
    for h in range(N_HEADS):
        wuvp = wuvp.at[h, :, h * HEAD_V:(h + 1) * HEAD_V].set(w_uv[h])
    wuvp = wuvp.astype(BF16)

    pa, gb, q_hm, qi_hm, kv, ki, wi = _inproj(
        x2d, row(norm1_g), wuv, wcq, wmisc, wga, wgb, row(q_lat_g), wuq, qn_g,
        row(kv_norm_g), wqi, kidx_g, row(gmlp_v_g), w_spatial.astype(F32),
        b_spatial.T.astype(F32), w_proj_a.astype(BF16))

    yb = _dsa(q_hm, qi_hm, wi, kv, ki, wuvp, batch, seq)

    wr = jnp.pad(w_router, ((0, 0), (0, LANES - N_EXPERTS))).astype(F32)
    wr_hi = wr.astype(BF16)
    wr = jnp.concatenate([wr_hi, (wr - wr_hi.astype(F32)).astype(BF16)], axis=1)
    br = jnp.pad(b_router, (0, LANES - N_EXPERTS)).reshape(1, LANES).astype(F32)
    x2, xn2, te, gate, rank, cnt = _merge(
        x2d, pa, gb, yb, w_proj_b.astype(BF16), w_out.astype(BF16), row(norm2_g), wr, br)

    counts = cnt[0, :N_EXPERTS].astype(I32)
    padded = (counts + MOE_BLOCK - 1) // MOE_BLOCK * MOE_BLOCK
    pad_end = jnp.cumsum(padded)
    pstart = (pad_end - padded).astype(I32)
    n_blocks = (t * TOP_K) // MOE_BLOCK + N_EXPERTS
    blk_start = jnp.arange(n_blocks, dtype=I32) * MOE_BLOCK
    blk_e = jnp.minimum(jnp.sum((pad_end[None, :] <= blk_start[:, None]).astype(I32), axis=1),
                        N_EXPERTS - 1).astype(I32)
    n_used = (pad_end[N_EXPERTS - 1:] // MOE_BLOCK).astype(I32)
    te_flat = te[:, :TOP_K].reshape(t * TOP_K)
    rank_flat = rank[:, :TOP_K].reshape(t * TOP_K)

    xs = _dispatch(pstart, counts, n_used, te_flat, rank_flat, xn2, n_blocks * MOE_BLOCK)
    ys = _moe(blk_e, n_used, xs, w_exp1.astype(F32), b_exp1.reshape(N_EXPERTS, 1, -1).astype(F32),
              w_exp2.astype(F32), b_exp2.reshape(N_EXPERTS, 1, -1).astype(F32))
    out = _combine(pstart, te_flat, rank_flat, x2, gate, ys)
    return out.reshape(batch, seq, d)
```
